```python
import jax, jax.numpy as jnp
from jax import lax
import numpy as np

D_MODEL = 1024
BATCH = 32
SEQ = 2048
DEPTH = 4

N_MIXERS = 2
N_GLA_LAYERS = (DEPTH + 1) // 2
N_SB_LAYERS = DEPTH // 2
GLA_HEADS = 4
GLA_KD = D_MODEL // 2
GLA_VD = D_MODEL
GLA_HK = GLA_KD // GLA_HEADS
GLA_HV = GLA_VD // GLA_HEADS
GLA_GATE_RANK = 16
GLA_TAU = 16.0
GLA_CHUNK = 64
GLA_IN = 2 * GLA_KD + 2 * GLA_VD + GLA_GATE_RANK
SB_HEAD_DIM = 64
SB_HEADS = D_MODEL // SB_HEAD_DIM
SB_QBLOCK = 128
D_FF = ((8 * D_MODEL // 3 + 255) // 256) * 256
NORM_EPS = 1e-6

kernel_name = 'hybrid_gla_stickbreaking_swiglu'


def _rmsnorm(x, g):
    xf = x.astype(jnp.float32)
    y = xf * lax.rsqrt(jnp.mean(xf * xf, axis=-1, keepdims=True) + NORM_EPS)
    return (y * g.astype(jnp.float32)).astype(x.dtype)


def _gla_mixer(h, w_in, w_gate2, b_gate, norm_g, w_out):
    B, S, _ = h.shape
    n_chunks = S // GLA_CHUNK
    proj = h @ w_in
    q, k, v, r, g_lr = jnp.split(proj, [GLA_KD, 2 * GLA_KD, 2 * GLA_KD + GLA_VD, 2 * GLA_KD + 2 * GLA_VD], axis=-1)
    log_a = jax.nn.log_sigmoid((g_lr @ w_gate2 + b_gate).astype(jnp.float32)) / GLA_TAU

    def to_chunks(t, d):
        return t.astype(jnp.float32).reshape(B, n_chunks, GLA_CHUNK, GLA_HEADS, d).transpose(1, 0, 3, 2, 4)

    qc = to_chunks(q, GLA_HK) * (GLA_HK ** -0.5)
    kc = to_chunks(k, GLA_HK)
    vc = to_chunks(v, GLA_HV)
    ac = to_chunks(log_a, GLA_HK)
    causal = jnp.tril(jnp.ones((GLA_CHUNK, GLA_CHUNK), dtype=bool))

    def step(state, inp):
        qb, kb, vb, ab = inp
        bcum = jnp.cumsum(ab, axis=2)
        diff = bcum[:, :, :, None, :] - bcum[:, :, None, :, :]
        decay = jnp.exp(jnp.where(causal[:, :, None], diff, -jnp.inf))
        scores = jnp.einsum('bhtk,bhsk,bhtsk->bhts', qb, kb, decay)
        o = jnp.einsum('bhts,bhsv->bhtv', scores, vb) + jnp.einsum('bhtk,bhkv->bhtv', qb * jnp.exp(bcum), state)
        blast = bcum[:, :, -1:, :]
        state = state * jnp.exp(blast[:, :, 0, :, None]) + jnp.einsum('bhsk,bhsv->bhkv', kb * jnp.exp(blast - bcum), vb)
        return state, o

    state0 = jnp.zeros((B, GLA_HEADS, GLA_HK, GLA_HV), jnp.float32)
    _, o = lax.scan(step, state0, (qc, kc, vc, ac))
    o = o.transpose(1, 0, 3, 2, 4).reshape(B, S, GLA_HEADS, GLA_HV)
    o = _rmsnorm(o, norm_g).reshape(B, S, GLA_VD)
    o = o * jax.nn.silu(r.astype(jnp.float32))
    return o.astype(h.dtype) @ w_out


def _sb_mixer(h, w_in, q_norm_g, k_norm_g, w_out):
    B, S, _ = h.shape
    q, k, v = jnp.split(h @ w_in, 3, axis=-1)
    heads = lambda t: t.reshape(B, S, SB_HEADS, SB_HEAD_DIM).transpose(0, 2, 1, 3)
    q = _rmsnorm(heads(q), q_norm_g).astype(jnp.float32)
    k = _rmsnorm(heads(k), k_norm_g).astype(jnp.float32)
    v = heads(v).astype(jnp.float32)
    scale = SB_HEAD_DIM ** -0.5
    outs = []
    for blk in range(S // SB_QBLOCK):
        t0 = blk * SB_QBLOCK
        t1 = t0 + SB_QBLOCK
        kb = k[:, :, :t1]
        vb = v[:, :, :t1]
        z = jnp.einsum('bhtd,bhsd->bhts', q[:, :, t0:t1], kb) * scale
        t_idx = t0 + jnp.arange(SB_QBLOCK)[:, None]
        s_idx = jnp.arange(t1)[None, :]
        mask = s_idx < t_idx
        log_1mb = jnp.where(mask, jax.nn.log_sigmoid(-z), 0.0)
        rem = lax.cumsum(log_1mb, axis=3, reverse=True) - log_1mb
        w = jnp.where(mask, jnp.exp(jax.nn.log_sigmoid(z) + rem), 0.0)
        outs.append(jnp.einsum('bhts,bhsd->bhtd', w, vb))
    o = jnp.concatenate(outs, axis=2).transpose(0, 2, 1, 3).reshape(B, S, D_MODEL)
    return o.astype(h.dtype) @ w_out


def _swiglu(h, w_gate_up, w_down):
    g, u = jnp.split(h @ w_gate_up, 2, axis=-1)
    return (jax.nn.silu(g) * u) @ w_down


def setup_inputs(seed: int = 0) -> dict:
    key = jax.random.key(seed)
    ks = jax.random.split(key, 16)
    nrm = lambda k, shape, fan_in: jax.random.normal(k, shape, jnp.float32) * (fan_in ** -0.5)
    gain = lambda k, shape: 1.0 + 0.02 * jax.random.normal(k, shape, jnp.float32)
    return {
        'x': jax.random.normal(ks[0], (BATCH, SEQ, D_MODEL), jnp.float32),
        'attn_norm_g': gain(ks[1], (DEPTH, D_MODEL)),
        'ffn_norm_g': gain(ks[2], (DEPTH, D_MODEL)),
        'gla_w_in': nrm(ks[3], (N_GLA_LAYERS, D_MODEL, GLA_IN), D_MODEL),
        'gla_w_gate2': nrm(ks[4], (N_GLA_LAYERS, GLA_GATE_RANK, GLA_KD), GLA_GATE_RANK),
        'gla_b_gate': 0.01 * jax.random.normal(ks[5], (N_GLA_LAYERS, GLA_KD), jnp.float32),
        'gla_norm_g': gain(ks[6], (N_GLA_LAYERS, GLA_HV)),
        'gla_w_out': nrm(ks[7], (N_GLA_LAYERS, GLA_VD, D_MODEL), GLA_VD),
        'sb_w_in': nrm(ks[8], (N_SB_LAYERS, D_MODEL, 3 * D_MODEL), D_MODEL),
        'sb_q_norm_g': gain(ks[9], (N_SB_LAYERS, SB_HEAD_DIM)),
        'sb_k_norm_g': gain(ks[10], (N_SB_LAYERS, SB_HEAD_DIM)),
        'sb_w_out': nrm(ks[11], (N_SB_LAYERS, D_MODEL, D_MODEL), D_MODEL),
        'ffn_w_gate_up': nrm(ks[12], (DEPTH, D_MODEL, 2 * D_FF), D_MODEL),
        'ffn_w_down': nrm(ks[13], (DEPTH, D_FF, D_MODEL), D_FF),
    }


def reference(x, attn_norm_g, ffn_norm_g, gla_w_in, gla_w_gate2, gla_b_gate, gla_norm_g, gla_w_out,
              sb_w_in, sb_q_norm_g, sb_k_norm_g, sb_w_out, ffn_w_gate_up, ffn_w_down):
    for i in range(DEPTH):
        h = _rmsnorm(x, attn_norm_g[i])
        j = i // N_MIXERS
        if i % N_MIXERS == 0:
            x = x + _gla_mixer(h, gla_w_in[j], gla_w_gate2[j], gla_b_gate[j], gla_norm_g[j], gla_w_out[j])
        else:
            x = x + _sb_mixer(h, sb_w_in[j], sb_q_norm_g[j], sb_k_norm_g[j], sb_w_out[j])
        h = _rmsnorm(x, ffn_norm_g[i])
        x = x + _swiglu(h, ffn_w_gate_up[i], ffn_w_down[i])
    return x
```

```python
import functools

import jax
import jax.numpy as jnp
from jax import lax
from jax.experimental import pallas as pl
from jax.experimental.pallas import tpu as pltpu

F32 = jnp.float32
BF16 = jnp.bfloat16

NORM_EPS = 1e-6
GLA_HEADS = 4
GLA_TAU = 16.0
GLA_GATE_RANK = 16
SB_HEAD_DIM = 64

LANES = 128
ROW_TILE = 512
GLA_CHUNK = 128
SB_BLOCK = 128
VMEM_LIMIT = 56 * 1024 * 1024

_NT = (((1,), (1,)), ((), ()))
_TN = (((0,), (0,)), ((), ()))


def _rmsnorm(xf, g):
    return xf * lax.rsqrt(jnp.mean(xf * xf, axis=-1, keepdims=True) + NORM_EPS) * g


def _split_bf16(x):
    hi = x.astype(BF16)
    lo = (x - hi.astype(F32)).astype(BF16)
    return hi, lo


def _params(n_parallel):
    return pltpu.CompilerParams(
        dimension_semantics=("parallel",) * n_parallel,
        vmem_limit_bytes=VMEM_LIMIT,
    )


def _resident(shape):
    return pl.BlockSpec(shape, lambda *_: (0,) * len(shape), pipeline_mode=pl.Buffered(1))


def _rows(width):
    return pl.BlockSpec((ROW_TILE, width), lambda i: (i, 0))


def _ffn_body(x_ref, g_ref, wgu_ref, wd_ref, o_ref, *, d_ff):
    x = x_ref[...]
    h = _rmsnorm(x, g_ref[...]).astype(BF16)
    gu = jnp.dot(h, wgu_ref[...], preferred_element_type=F32)
    gate = gu[:, :d_ff]
    up = gu[:, d_ff:]
    act = (gate * jax.nn.sigmoid(gate) * up).astype(BF16)
    o_ref[...] = x + jnp.dot(act, wd_ref[...], preferred_element_type=F32)


def _ffn(x, g, w_gate_up, w_down):
    t, d = x.shape
    d_ff = w_down.shape[0]
    return pl.pallas_call(
        functools.partial(_ffn_body, d_ff=d_ff),
        out_shape=jax.ShapeDtypeStruct((t, d), F32),
        grid=(t // ROW_TILE,),
        in_specs=[_rows(d), _resident((1, d)), _resident((d, 2 * d_ff)), _resident((d_ff, d))],
        out_specs=_rows(d),
        compiler_params=_params(1),
        name="ffn",
    )(x, g, w_gate_up, w_down)


def _outproj_body(x_ref, o_ref, w_ref, y_ref):
    y_ref[...] = x_ref[...] + jnp.dot(o_ref[...], w_ref[...], preferred_element_type=F32)


def _outproj(x, o, w):
    t, d = x.shape
    return pl.pallas_call(
        _outproj_body,
        out_shape=jax.ShapeDtypeStruct((t, d), F32),
        grid=(t // ROW_TILE,),
        in_specs=[_rows(d), _rows(o.shape[1]), _resident(w.shape)],
        out_specs=_rows(d),
        compiler_params=_params(1),
        name="outproj",
    )(x, o, w)


def _sb_inproj_body(x_ref, g_ref, w_ref, q_ref, k_ref, v_ref):
    d = x_ref.shape[1]
    h = _rmsnorm(x_ref[...], g_ref[...]).astype(BF16)
    qkv = jnp.dot(h, w_ref[...], preferred_element_type=F32)
    q_ref[...] = qkv[:, :d].astype(BF16)
    k_ref[...] = qkv[:, d:2 * d].astype(BF16)
    v_ref[...] = qkv[:, 2 * d:].astype(BF16)


def _sb_inproj(x, g, w):
    t, d = x.shape
    out = jax.ShapeDtypeStruct((t, d), BF16)
    return pl.pallas_call(
        _sb_inproj_body,
        out_shape=(out, out, out),
        grid=(t // ROW_TILE,),
        in_specs=[_rows(d), _resident((1, d)), _resident((d, 3 * d))],
        out_specs=(_rows(d), _rows(d), _rows(d)),
        compiler_params=_params(1),
        name="sb_inproj",
    )(x, g, w)


def _sb_attn_body(q_ref, k_ref, v_ref, gq_ref, gk_ref, o_ref, qn_ref, kn_ref):
    seq = q_ref.shape[1]
    blk = SB_BLOCK
    lane = lax.broadcasted_iota(jnp.int32, (1, LANES), 1)
    first = lane < SB_HEAD_DIM

    def head_norm(t_ref, g_ref, scale):
        t = t_ref[0].astype(F32)
        sq = t * t
        ss_a = jnp.sum(jnp.where(first, sq, 0.0), axis=-1, keepdims=True)
        ss_b = jnp.sum(jnp.where(first, 0.0, sq), axis=-1, keepdims=True)
        inv = jnp.where(first,
                        lax.rsqrt(ss_a / SB_HEAD_DIM + NORM_EPS),
                        lax.rsqrt(ss_b / SB_HEAD_DIM + NORM_EPS))
        return (t * inv * g_ref[...] * scale).astype(BF16)

    qn_ref[...] = head_norm(q_ref, gq_ref, SB_HEAD_DIM ** -0.5)
    kn_ref[...] = head_norm(k_ref, gk_ref, 1.0)

    row = lax.broadcasted_iota(jnp.int32, (blk, blk), 0)
    col = lax.broadcasted_iota(jnp.int32, (blk, blk), 1)
    strict = col < row
    after = (row > col).astype(BF16)

    def one_head(qm, kk, vv, rem, acc, diag):
        z = lax.dot_general(qm, kk, _NT, preferred_element_type=F32)
        sp = jnp.maximum(z, 0.0) + jnp.log(1.0 + jnp.exp(-jnp.abs(z)))
        spm = jnp.where(strict, sp, 0.0) if diag else sp
        hi, lo = _split_bf16(spm)
        later = (jnp.dot(hi, after, preferred_element_type=F32)
                 + jnp.dot(lo, after, preferred_element_type=F32))
        w = jnp.exp(z - sp - later - rem)
        if diag:
            w = jnp.where(strict, w, 0.0)
        acc = acc + jnp.dot(w.astype(BF16), vv, preferred_element_type=F32)
        rem = rem + jnp.sum(spm, axis=-1, keepdims=True)
        return rem, acc

    def q_block(i, _):
        qs = pl.multiple_of(i * blk, blk)
        qi = qn_ref[pl.ds(qs, blk), :]
        qa = jnp.where(first, qi, jnp.zeros_like(qi))
        qb = jnp.where(first, jnp.zeros_like(qi), qi)

        def step(kb, carry, diag):
            ks = pl.multiple_of(kb * blk, blk)
            kk = kn_ref[pl.ds(ks, blk), :]
            vv = v_ref[0, pl.ds(ks, blk), :]
            rem_a, acc_a, rem_b, acc_b = carry
            rem_a, acc_a = one_head(qa, kk, vv, rem_a, acc_a, diag)
            rem_b, acc_b = one_head(qb, kk, vv, rem_b, acc_b, diag)
            return rem_a, acc_a, rem_b, acc_b

        zero_r = jnp.zeros((blk, 1), F32)
        zero_a = jnp.zeros((blk, LANES), F32)
        carry = step(i, (zero_r, zero_a, zero_r, zero_a), True)
        carry = lax.fori_loop(0, i, lambda j, c: step(i - 1 - j, c, False), carry)
        _, acc_a, _, acc_b = carry
        o_ref[0, pl.ds(qs, blk), :] = jnp.where(first, acc_a, acc_b).astype(BF16)
        return 0

    lax.fori_loop(0, seq // blk, q_block, 0)


def _sb_attn(q, k, v, gq, gk):
    b, s, d = q.shape
    pairs = d // LANES
    blockspec = pl.BlockSpec((1, s, LANES), lambda i, j: (i, 0, j))
    gain = pl.BlockSpec((1, LANES), lambda i, j: (0, 0))
    return pl.pallas_call(
        _sb_attn_body,
        out_shape=jax.ShapeDtypeStruct((b, s, d), BF16),
        grid=(b, pairs),
        in_specs=[blockspec, blockspec, blockspec, gain, gain],
        out_specs=blockspec,
        scratch_shapes=[pltpu.VMEM((s, LANES), BF16), pltpu.VMEM((s, LANES), BF16)],
        compiler_params=_params(2),
        name="sb_attn",
    )(q, k, v, gq, gk)


def _sb_layer(x, batch, attn_g, w_in, gq, gk, w_out):
    t, d = x.shape
    q, k, v = _sb_inproj(x, attn_g, w_in)
    shape3 = (batch, t // batch, d)
    reps = LANES // SB_HEAD_DIM
    o = _sb_attn(q.reshape(shape3), k.reshape(shape3), v.reshape(shape3),
                 jnp.tile(gq, reps)[None, :], jnp.tile(gk, reps)[None, :])
    return _outproj(x, o.reshape(t, d), w_out)


def _gla_inproj_body(x_ref, g_ref, w_ref, wg1_ref, wg2_ref, bg_ref,
                     q_ref, k_ref, v_ref, r_ref, la_ref, *, kd, vd, q_scale):
    h = _rmsnorm(x_ref[...], g_ref[...]).astype(BF16)
    proj = jnp.dot(h, w_ref[...], preferred_element_type=F32)
    q_ref[...] = (proj[:, :kd] * q_scale).astype(BF16)
    k_ref[...] = proj[:, kd:2 * kd].astype(BF16)
    v_ref[...] = proj[:, 2 * kd:2 * kd + vd].astype(BF16)
    r_ref[...] = proj[:, 2 * kd + vd:].astype(BF16)
    low_rank = jnp.dot(h, wg1_ref[...], preferred_element_type=F32).astype(BF16)
    gate = jnp.dot(low_rank, wg2_ref[...], preferred_element_type=F32) + bg_ref[...]
    softplus_neg = jnp.maximum(-gate, 0.0) + jnp.log(1.0 + jnp.exp(-jnp.abs(gate)))
    la_ref[...] = -softplus_neg / GLA_TAU


def _gla_inproj(x, g, w, wg1, wg2, bg, kd, vd):
    t, d = x.shape
    body = functools.partial(_gla_inproj_body, kd=kd, vd=vd,
                             q_scale=(kd // GLA_HEADS) ** -0.5)
    return pl.pallas_call(
        body,
        out_shape=(jax.ShapeDtypeStruct((t, kd), BF16), jax.ShapeDtypeStruct((t, kd), BF16),
                   jax.ShapeDtypeStruct((t, vd), BF16), jax.ShapeDtypeStruct((t, vd), BF16),
                   jax.ShapeDtypeStruct((t, kd), F32)),
        grid=(t // ROW_TILE,),
        in_specs=[_rows(d), _resident((1, d)), _resident(w.shape), _resident(wg1.shape),
                  _resident(wg2.shape), _resident((1, kd))],
        out_specs=(_rows(kd), _rows(kd), _rows(vd), _rows(vd), _rows(kd)),
        compiler_params=_params(1),
        name="gla_inproj",
    )(x, g, w, wg1, wg2, bg)


def _gla_scan_body(q_ref, k_ref, la_ref, v_ref, o_ref, state_ref):
    seq = q_ref.shape[1]
    c = GLA_CHUNK
    row = lax.broadcasted_iota(jnp.int32, (c, c), 0)
    col = lax.broadcasted_iota(jnp.int32, (c, c), 1)
    causal = col <= row
    prefix = causal.astype(BF16)
    state_ref[...] = jnp.zeros_like(state_ref)

    def chunk(n, _):
        s0 = pl.multiple_of(n * c, c)
        la_hi, la_lo = _split_bf16(la_ref[0, pl.ds(s0, c), :])
        cum = (jnp.dot(prefix, la_hi, preferred_element_type=F32)
               + jnp.dot(prefix, la_lo, preferred_element_type=F32))
        qf = q_ref[0, pl.ds(s0, c), :].astype(F32)
        kf = k_ref[0, pl.ds(s0, c), :].astype(F32)
        vv = v_ref[0, pl.ds(s0, c), :]
        qd = (qf * jnp.exp(cum)).astype(BF16)
        kd = (kf * jnp.exp(-cum)).astype(BF16)
        scores = lax.dot_general(qd, kd, _NT, preferred_element_type=F32)
        scores = jnp.where(causal, scores, 0.0).astype(BF16)
        state = state_ref[...]
        o_ref[0, pl.ds(s0, c), :] = (
            jnp.dot(scores, vv, preferred_element_type=F32)
            + lax.dot_general(qd, state.astype(BF16), _NT, preferred_element_type=F32))
        last = cum[c - 1:c, :]
        k_end = (kf * jnp.exp(last - cum)).astype(BF16)
        state_ref[...] = state * jnp.exp(last) + lax.dot_general(vv, k_end, _TN,
                                                                  preferred_element_type=F32)
        return 0

    lax.fori_loop(0, seq // c, chunk, 0)


def _gla_scan(q, k, la, v):
    b, s, kd = q.shape
    vd = v.shape[2]
    hk = kd // GLA_HEADS
    hv = vd // GLA_HEADS
    key_spec = pl.BlockSpec((1, s, hk), lambda i, j: (i, 0, j))
    val_spec = pl.BlockSpec((1, s, hv), lambda i, j: (i, 0, j))
    return pl.pallas_call(
        _gla_scan_body,
        out_shape=jax.ShapeDtypeStruct((b, s, vd), F32),
        grid=(b, GLA_HEADS),
        in_specs=[key_spec, key_spec, key_spec, val_spec],
        out_specs=val_spec,
        scratch_shapes=[pltpu.VMEM((hv, hk), F32)],
        compiler_params=_params(2),
        name="gla_scan",
    )(q, k, la, v)


def _gla_outproj_body(x_ref, o_ref, r_ref, ng_ref, w_ref, y_ref, *, hv):
    o = o_ref[...]
    r = r_ref[...].astype(F32)
    ng = ng_ref[...]
    parts = []
    for h in range(GLA_HEADS):
        sl = slice(h * hv, (h + 1) * hv)
        rh = r[:, sl]
        parts.append((_rmsnorm(o[:, sl], ng) * (rh * jax.nn.sigmoid(rh))).astype(BF16))
    gated = jnp.concatenate(parts, axis=-1)
    y_ref[...] = x_ref[...] + jnp.dot(gated, w_ref[...], preferred_element_type=F32)


def _gla_outproj(x, o, r, ng, w):
    t, d = x.shape
    vd = o.shape[1]
    return pl.pallas_call(
        functools.partial(_gla_outproj_body, hv=vd // GLA_HEADS),
        out_shape=jax.ShapeDtypeStruct((t, d), F32),
        grid=(t // ROW_TILE,),
        in_specs=[_rows(d), _rows(vd), _rows(vd), _resident(ng.shape), _resident(w.shape)],
        out_specs=_rows(d),
        compiler_params=_params(1),
        name="gla_outproj",
    )(x, o, r, ng, w)


def _gla_layer(x, batch, attn_g, w_in, w_gate2, b_gate, norm_g, w_out):
    t, d = x.shape
    kd = w_gate2.shape[1]
    vd = w_out.shape[0]
    main = 2 * kd + 2 * vd
    wg1 = jnp.pad(w_in[:, main:], ((0, 0), (0, LANES - GLA_GATE_RANK)))
    wg2 = jnp.pad(w_gate2, ((0, LANES - GLA_GATE_RANK), (0, 0)))
    q, k, v, r, la = _gla_inproj(x, attn_g, w_in[:, :main], wg1, wg2, b_gate[None, :], kd, vd)
    s = t // batch
    o = _gla_scan(q.reshape(batch, s, kd), k.reshape(batch, s, kd), la.reshape(batch, s, kd),
                  v.reshape(batch, s, vd))
    return _gla_outproj(x, o.reshape(t, vd), r, norm_g[None, :], w_out)


def kernel(x, attn_norm_g, ffn_norm_g, gla_w_in, gla_w_gate2, gla_b_gate, gla_norm_g, gla_w_out,
           sb_w_in, sb_q_norm_g, sb_k_norm_g, sb_w_out, ffn_w_gate_up, ffn_w_down):
    batch, seq, d = x.shape
    depth = attn_norm_g.shape[0]
    assert (batch * seq) % ROW_TILE == 0 and seq % GLA_CHUNK == 0 and seq % SB_BLOCK == 0
    bf = lambda w: w.astype(BF16)
    xt = x.reshape(batch * seq, d)
    for i in range(depth):
        j = i // 2
        if i % 2 == 0:
            xt = _gla_layer(xt, batch, attn_norm_g[i][None, :], bf(gla_w_in[j]), bf(gla_w_gate2[j]),
                            gla_b_gate[j], gla_norm_g[j], bf(gla_w_out[j]))
        else:
            xt = _sb_layer(xt, batch, attn_norm_g[i][None, :], bf(sb_w_in[j]),
                           sb_q_norm_g[j], sb_k_norm_g[j], bf(sb_w_out[j]))
        xt = _ffn(xt, ffn_norm_g[i][None, :], bf(ffn_w_gate_up[i]), bf(ffn_w_down[i]))
    return xt.reshape(batch, seq, d)
```

```python
import functools

import jax
import jax.numpy as jnp
from jax import lax
from jax.experimental import pallas as pl
from jax.experimental.pallas import tpu as pltpu

F32 = jnp.float32
BF16 = jnp.bfloat16

NORM_EPS = 1e-6
GLA_HEADS = 4
GLA_TAU = 16.0
GLA_GATE_RANK = 16
SB_HEAD_DIM = 64

LANES = 128
ROW_TILE = 512
GLA_CHUNK = 128
SB_BLOCK = 128
SB_BAND = 3
SB_ZERO_WEIGHT = 104.0
VMEM_LIMIT = 56 * 1024 * 1024

_NT = (((1,), (1,)), ((), ()))
_TN = (((0,), (0,)), ((), ()))


def _rmsnorm(xf, g):
    return xf * lax.rsqrt(jnp.mean(xf * xf, axis=-1, keepdims=True) + NORM_EPS) * g


def _split_bf16(x):
    hi = x.astype(BF16)
    lo = (x - hi.astype(F32)).astype(BF16)
    return hi, lo


def _params(n_parallel):
    return pltpu.CompilerParams(
        dimension_semantics=("parallel",) * n_parallel,
        vmem_limit_bytes=VMEM_LIMIT,
    )


def _resident(shape):
    return pl.BlockSpec(shape, lambda *_: (0,) * len(shape), pipeline_mode=pl.Buffered(1))


def _rows(width):
    return pl.BlockSpec((ROW_TILE, width), lambda i: (i, 0))


def _ffn_body(x_ref, g_ref, wgu_ref, wd_ref, o_ref, *, d_ff):
    x = x_ref[...]
    h = _rmsnorm(x, g_ref[...]).astype(BF16)
    gu = jnp.dot(h, wgu_ref[...], preferred_element_type=F32)
    gate = gu[:, :d_ff]
    up = gu[:, d_ff:]
    act = (gate * jax.nn.sigmoid(gate) * up).astype(BF16)
    o_ref[...] = x + jnp.dot(act, wd_ref[...], preferred_element_type=F32)


def _ffn(x, g, w_gate_up, w_down):
    t, d = x.shape
    d_ff = w_down.shape[0]
    return pl.pallas_call(
        functools.partial(_ffn_body, d_ff=d_ff),
        out_shape=jax.ShapeDtypeStruct((t, d), F32),
        grid=(t // ROW_TILE,),
        in_specs=[_rows(d), _resident((1, d)), _resident((d, 2 * d_ff)), _resident((d_ff, d))],
        out_specs=_rows(d),
        compiler_params=_params(1),
        name="ffn",
    )(x, g, w_gate_up, w_down)


def _outproj_body(x_ref, o_ref, w_ref, y_ref):
    y_ref[...] = x_ref[...] + jnp.dot(o_ref[...], w_ref[...], preferred_element_type=F32)


def _outproj(x, o, w):
    t, d = x.shape
    return pl.pallas_call(
        _outproj_body,
        out_shape=jax.ShapeDtypeStruct((t, d), F32),
        grid=(t // ROW_TILE,),
        in_specs=[_rows(d), _rows(o.shape[1]), _resident(w.shape)],
        out_specs=_rows(d),
        compiler_params=_params(1),
        name="outproj",
    )(x, o, w)


def _sb_inproj_body(x_ref, g_ref, w_ref, q_ref, k_ref, v_ref):
    d = x_ref.shape[1]
    h = _rmsnorm(x_ref[...], g_ref[...]).astype(BF16)
    qkv = jnp.dot(h, w_ref[...], preferred_element_type=F32)
    q_ref[...] = qkv[:, :d].astype(BF16)
    k_ref[...] = qkv[:, d:2 * d].astype(BF16)
    v_ref[...] = qkv[:, 2 * d:].astype(BF16)


def _sb_inproj(x, g, w):
    t, d = x.shape
    out = jax.ShapeDtypeStruct((t, d), BF16)
    return pl.pallas_call(
        _sb_inproj_body,
        out_shape=(out, out, out),
        grid=(t // ROW_TILE,),
        in_specs=[_rows(d), _resident((1, d)), _resident((d, 3 * d))],
        out_specs=(_rows(d), _rows(d), _rows(d)),
        compiler_params=_params(1),
        name="sb_inproj",
    )(x, g, w)


def _softplus(z):
    return jnp.maximum(z, 0.0) + jnp.log(1.0 + jnp.exp(-jnp.abs(z)))


def _sb_attn_body(q_ref, k_ref, v_ref, gq_ref, gk_ref, o_ref,
                  qa_ref, qb_ref, kn_ref, acc_a_ref, acc_b_ref, rem_a_ref, rem_b_ref):
    seq = q_ref.shape[1]
    blk = SB_BLOCK
    nb = seq // blk
    lane = lax.broadcasted_iota(jnp.int32, (1, LANES), 1)
    first = lane < SB_HEAD_DIM

    def head_norm(t_ref, g_ref, scale):
        t = t_ref[0].astype(F32)
        sq = t * t
        ss_a = jnp.sum(jnp.where(first, sq, 0.0), axis=-1, keepdims=True)
        ss_b = jnp.sum(jnp.where(first, 0.0, sq), axis=-1, keepdims=True)
        inv = jnp.where(first,
                        lax.rsqrt(ss_a / SB_HEAD_DIM + NORM_EPS),
                        lax.rsqrt(ss_b / SB_HEAD_DIM + NORM_EPS))
        return (t * inv * g_ref[...] * scale).astype(BF16)

    qn = head_norm(q_ref, gq_ref, SB_HEAD_DIM ** -0.5)
    qa_ref[...] = jnp.where(first, qn, jnp.zeros_like(qn))
    qb_ref[...] = jnp.where(first, jnp.zeros_like(qn), qn)
    kn_ref[...] = head_norm(k_ref, gk_ref, 1.0)

    row = lax.broadcasted_iota(jnp.int32, (blk, blk), 0)
    col = lax.broadcasted_iota(jnp.int32, (blk, blk), 1)
    strict = col < row
    row2 = lax.broadcasted_iota(jnp.int32, (blk, 2 * blk), 0)
    col2 = lax.broadcasted_iota(jnp.int32, (blk, 2 * blk), 1)
    later_and_total = ((row2 > col2) | (col2 >= blk)).astype(BF16)

    def mask_first_block(x):
        masked = jnp.where(strict, x[:blk], 0.0)
        return masked if x.shape[0] == blk else jnp.concatenate([masked, x[blk:]], axis=0)

    def block_step(q_rows, kk, vv, rem, diagonal):
        z = lax.dot_general(q_rows, kk, _NT, preferred_element_type=F32)
        sp = _softplus(z)
        log_beta = z - sp
        spm = mask_first_block(sp) if diagonal else sp
        sums = jnp.dot(spm.astype(BF16), later_and_total, preferred_element_type=F32)
        w = jnp.exp(log_beta - sums[:, :blk] - rem)
        if diagonal:
            w = mask_first_block(w)
        return jnp.dot(w.astype(BF16), vv, preferred_element_type=F32), sums[:, blk:]

    heads = ((qa_ref, acc_a_ref, rem_a_ref), (qb_ref, acc_b_ref, rem_b_ref))

    carried = [[], []]
    for kb in range(nb - 1, -1, -1):
        m = min(SB_BAND, nb - kb)
        r0 = kb * blk
        kk = kn_ref[r0:r0 + blk, :]
        vv = v_ref[0, r0:r0 + blk, :]
        for h, (qsel_ref, acc_ref, rem_ref) in enumerate(heads):
            rems = [jnp.zeros((blk, blk), F32)] + carried[h][:m - 1]
            out, tot = block_step(qsel_ref[r0:r0 + m * blk, :], kk, vv,
                                  jnp.concatenate(rems, axis=0), True)
            acc_ref[r0:r0 + blk, :] = out[:blk]
            if m > 1:
                acc_ref[r0 + blk:r0 + m * blk, :] += out[blk:]
            rems = [rems[j] + tot[j * blk:(j + 1) * blk] for j in range(m)]
            if m == SB_BAND:
                rem_ref[r0 + (m - 1) * blk:r0 + m * blk, :] = rems[m - 1]
            carried[h] = rems[:SB_BAND - 1]

    def need_more(rem_a, rem_b):
        return (jnp.minimum(jnp.min(rem_a), jnp.min(rem_b)) < SB_ZERO_WEIGHT).astype(jnp.int32)

    def continuation(i, _):
        qs = pl.multiple_of(i * blk, blk)

        def body(c):
            kb, _, rem_a, rem_b = c
            ks = pl.multiple_of(kb * blk, blk)
            kk = kn_ref[pl.ds(ks, blk), :]
            vv = v_ref[0, pl.ds(ks, blk), :]
            new_rems = []
            for (qsel_ref, acc_ref, _), rem in zip(heads, (rem_a, rem_b)):
                out, tot = block_step(qsel_ref[pl.ds(qs, blk), :], kk, vv, rem, False)
                acc_ref[pl.ds(qs, blk), :] += out
                new_rems.append(rem + tot)
            return kb - 1, need_more(*new_rems), new_rems[0], new_rems[1]

        rem_a = rem_a_ref[pl.ds(qs, blk), :]
        rem_b = rem_b_ref[pl.ds(qs, blk), :]
        lax.while_loop(lambda c: (c[0] >= 0) & (c[1] > 0), body,
                       (i - SB_BAND, need_more(rem_a, rem_b), rem_a, rem_b))
        return 0

    lax.fori_loop(SB_BAND, nb, continuation, 0)
    o_ref[0] = jnp.where(first, acc_a_ref[...], acc_b_ref[...]).astype(BF16)


def _sb_attn(q, k, v, gq, gk):
    b, s, d = q.shape
    pairs = d // LANES
    blockspec = pl.BlockSpec((1, s, LANES), lambda i, j: (i, 0, j))
    gain = pl.BlockSpec((1, LANES), lambda i, j: (0, 0))
    return pl.pallas_call(
        _sb_attn_body,
        out_shape=jax.ShapeDtypeStruct((b, s, d), BF16),
        grid=(b, pairs),
        in_specs=[blockspec, blockspec, blockspec, gain, gain],
        out_specs=blockspec,
        scratch_shapes=[pltpu.VMEM((s, LANES), BF16)] * 3 + [pltpu.VMEM((s, LANES), F32)] * 4,
        compiler_params=_params(2),
        name="sb_attn",
    )(q, k, v, gq, gk)


def _sb_layer(x, batch, attn_g, w_in, gq, gk, w_out):
    t, d = x.shape
    q, k, v = _sb_inproj(x, attn_g, w_in)
    shape3 = (batch, t // batch, d)
    reps = LANES // SB_HEAD_DIM
    o = _sb_attn(q.reshape(shape3), k.reshape(shape3), v.reshape(shape3),
                 jnp.tile(gq, reps)[None, :], jnp.tile(gk, reps)[None, :])
    return _outproj(x, o.reshape(t, d), w_out)


def _gla_inproj_body(x_ref, g_ref, w_ref, wg1_ref, wg2_ref, bg_ref,
                     q_ref, k_ref, v_ref, r_ref, la_ref, *, kd, vd, q_scale):
    h = _rmsnorm(x_ref[...], g_ref[...]).astype(BF16)
    proj = jnp.dot(h, w_ref[...], preferred_element_type=F32)
    q_ref[...] = (proj[:, :kd] * q_scale).astype(BF16)
    k_ref[...] = proj[:, kd:2 * kd].astype(BF16)
    v_ref[...] = proj[:, 2 * kd:2 * kd + vd].astype(BF16)
    r_ref[...] = proj[:, 2 * kd + vd:].astype(BF16)
    low_rank = jnp.dot(h, wg1_ref[...], preferred_element_type=F32).astype(BF16)
    gate = jnp.dot(low_rank, wg2_ref[...], preferred_element_type=F32) + bg_ref[...]
    softplus_neg = jnp.maximum(-gate, 0.0) + jnp.log(1.0 + jnp.exp(-jnp.abs(gate)))
    la_ref[...] = -softplus_neg / GLA_TAU


def _gla_inproj(x, g, w, wg1, wg2, bg, kd, vd):
    t, d = x.shape
    body = functools.partial(_gla_inproj_body, kd=kd, vd=vd,
                             q_scale=(kd // GLA_HEADS) ** -0.5)
    return pl.pallas_call(
        body,
        out_shape=(jax.ShapeDtypeStruct((t, kd), BF16), jax.ShapeDtypeStruct((t, kd), BF16),
                   jax.ShapeDtypeStruct((t, vd), BF16), jax.ShapeDtypeStruct((t, vd), BF16),
                   jax.ShapeDtypeStruct((t, kd), F32)),
        grid=(t // ROW_TILE,),
        in_specs=[_rows(d), _resident((1, d)), _resident(w.shape), _resident(wg1.shape),
                  _resident(wg2.shape), _resident((1, kd))],
        out_specs=(_rows(kd), _rows(kd), _rows(vd), _rows(vd), _rows(kd)),
        compiler_params=_params(1),
        name="gla_inproj",
    )(x, g, w, wg1, wg2, bg)


def _gla_scan_body(q_ref, k_ref, la_ref, v_ref, o_ref, state_ref):
    seq = q_ref.shape[1]
    c = GLA_CHUNK
    row = lax.broadcasted_iota(jnp.int32, (c, c), 0)
    col = lax.broadcasted_iota(jnp.int32, (c, c), 1)
    causal = col <= row
    prefix = causal.astype(BF16)
    state_ref[...] = jnp.zeros_like(state_ref)

    def chunk(n, _):
        s0 = pl.multiple_of(n * c, c)
        la_hi, la_lo = _split_bf16(la_ref[0, pl.ds(s0, c), :])
        cum = (jnp.dot(prefix, la_hi, preferred_element_type=F32)
               + jnp.dot(prefix, la_lo, preferred_element_type=F32))
        qf = q_ref[0, pl.ds(s0, c), :].astype(F32)
        kf = k_ref[0, pl.ds(s0, c), :].astype(F32)
        vv = v_ref[0, pl.ds(s0, c), :]
        qd = (qf * jnp.exp(cum)).astype(BF16)
        kd = (kf * jnp.exp(-cum)).astype(BF16)
        scores = lax.dot_general(qd, kd, _NT, preferred_element_type=F32)
        scores = jnp.where(causal, scores, 0.0).astype(BF16)
        state = state_ref[...]
        o_ref[0, pl.ds(s0, c), :] = (
            jnp.dot(scores, vv, preferred_element_type=F32)
            + lax.dot_general(qd, state.astype(BF16), _NT, preferred_element_type=F32))
        last = cum[c - 1:c, :]
        k_end = (kf * jnp.exp(last - cum)).astype(BF16)
        state_ref[...] = state * jnp.exp(last) + lax.dot_general(vv, k_end, _TN,
                                                                  preferred_element_type=F32)
        return 0

    lax.fori_loop(0, seq // c, chunk, 0)


def _gla_scan(q, k, la, v):
    b, s, kd = q.shape
    vd = v.shape[2]
    hk = kd // GLA_HEADS
    hv = vd // GLA_HEADS
    key_spec = pl.BlockSpec((1, s, hk), lambda i, j: (i, 0, j))
    val_spec = pl.BlockSpec((1, s, hv), lambda i, j: (i, 0, j))
    return pl.pallas_call(
        _gla_scan_body,
        out_shape=jax.ShapeDtypeStruct((b, s, vd), F32),
        grid=(b, GLA_HEADS),
        in_specs=[key_spec, key_spec, key_spec, val_spec],
        out_specs=val_spec,
        scratch_shapes=[pltpu.VMEM((hv, hk), F32)],
        compiler_params=_params(2),
        name="gla_scan",
    )(q, k, la, v)


def _gla_outproj_body(x_ref, o_ref, r_ref, ng_ref, w_ref, y_ref, *, hv):
    o = o_ref[...]
    r = r_ref[...].astype(F32)
    ng = ng_ref[...]
    parts = []
    for h in range(GLA_HEADS):
        sl = slice(h * hv, (h + 1) * hv)
        rh = r[:, sl]
        parts.append((_rmsnorm(o[:, sl], ng) * (rh * jax.nn.sigmoid(rh))).astype(BF16))
    gated = jnp.concatenate(parts, axis=-1)
    y_ref[...] = x_ref[...] + jnp.dot(gated, w_ref[...], preferred_element_type=F32)


def _gla_outproj(x, o, r, ng, w):
    t, d = x.shape
    vd = o.shape[1]
    return pl.pallas_call(
        functools.partial(_gla_outproj_body, hv=vd // GLA_HEADS),
        out_shape=jax.ShapeDtypeStruct((t, d), F32),
        grid=(t // ROW_TILE,),
        in_specs=[_rows(d), _rows(vd), _rows(vd), _resident(ng.shape), _resident(w.shape)],
        out_specs=_rows(d),
        compiler_params=_params(1),
        name="gla_outproj",
    )(x, o, r, ng, w)


def _gla_layer(x, batch, attn_g, w_in, w_gate2, b_gate, norm_g, w_out):
    t, d = x.shape
    kd = w_gate2.shape[1]
    vd = w_out.shape[0]
    main = 2 * kd + 2 * vd
    wg1 = jnp.pad(w_in[:, main:], ((0, 0), (0, LANES - GLA_GATE_RANK)))
    wg2 = jnp.pad(w_gate2, ((0, LANES - GLA_GATE_RANK), (0, 0)))
    q, k, v, r, la = _gla_inproj(x, attn_g, w_in[:, :main], wg1, wg2, b_gate[None, :], kd, vd)
    s = t // batch
    o = _gla_scan(q.reshape(batch, s, kd), k.reshape(batch, s, kd), la.reshape(batch, s, kd),
                  v.reshape(batch, s, vd))
    return _gla_outproj(x, o.reshape(t, vd), r, norm_g[None, :], w_out)


def kernel(x, attn_norm_g, ffn_norm_g, gla_w_in, gla_w_gate2, gla_b_gate, gla_norm_g, gla_w_out,
           sb_w_in, sb_q_norm_g, sb_k_norm_g, sb_w_out, ffn_w_gate_up, ffn_w_down):
    batch, seq, d = x.shape
    depth = attn_norm_g.shape[0]
    assert (batch * seq) % ROW_TILE == 0 and seq % GLA_CHUNK == 0 and seq % SB_BLOCK == 0
    bf = lambda w: w.astype(BF16)
    xt = x.reshape(batch * seq, d)
    for i in range(depth):
        j = i // 2
        if i % 2 == 0:
            xt = _gla_layer(xt, batch, attn_norm_g[i][None, :], bf(gla_w_in[j]), bf(gla_w_gate2[j]),
                            gla_b_gate[j], gla_norm_g[j], bf(gla_w_out[j]))
        else:
            xt = _sb_layer(xt, batch, attn_norm_g[i][None, :], bf(sb_w_in[j]),
                           sb_q_norm_g[j], sb_k_norm_g[j], bf(sb_w_out[j]))
        xt = _ffn(xt, ffn_norm_g[i][None, :], bf(ffn_w_gate_up[i]), bf(ffn_w_down[i]))
    return xt.reshape(batch, seq, d)
```

```python
import functools

import jax
import jax.numpy as jnp
from jax import lax
from jax.experimental import pallas as pl
from jax.experimental.pallas import tpu as pltpu

F32 = jnp.float32
BF16 = jnp.bfloat16

NORM_EPS = 1e-6
GLA_HEADS = 4
GLA_TAU = 16.0
GLA_GATE_RANK = 16
SB_HEAD_DIM = 64

LANES = 128
ROW_TILE = 512
GLA_CHUNK = 128
SB_BLOCK = 128
SB_BAND = 3
SB_GROUP = 4
SB_ZERO_WEIGHT = 104.0
VMEM_LIMIT = 56 * 1024 * 1024

_NT = (((1,), (1,)), ((), ()))
_TN = (((0,), (0,)), ((), ()))


def _rmsnorm(xf, g):
    return xf * lax.rsqrt(jnp.mean(xf * xf, axis=-1, keepdims=True) + NORM_EPS) * g


def _split_bf16(x):
    hi = x.astype(BF16)
    lo = (x - hi.astype(F32)).astype(BF16)
    return hi, lo


def _params(n_parallel):
    return pltpu.CompilerParams(
        dimension_semantics=("parallel",) * n_parallel,
        vmem_limit_bytes=VMEM_LIMIT,
    )


def _resident(shape):
    return pl.BlockSpec(shape, lambda *_: (0,) * len(shape), pipeline_mode=pl.Buffered(1))


def _rows(width):
    return pl.BlockSpec((ROW_TILE, width), lambda i: (i, 0))


def _swiglu_residual(x, g_ref, wgu_ref, wd_ref):
    d_ff = wd_ref.shape[0]
    h = _rmsnorm(x, g_ref[...]).astype(BF16)
    gu = jnp.dot(h, wgu_ref[...], preferred_element_type=F32)
    gate = gu[:, :d_ff]
    up = gu[:, d_ff:]
    act = (gate * jax.nn.sigmoid(gate) * up).astype(BF16)
    return x + jnp.dot(act, wd_ref[...], preferred_element_type=F32)


def _sb_out_ffn_body(x_ref, o_ref, wo_ref, g_ref, wgu_ref, wd_ref, y_ref):
    x = x_ref[...] + jnp.dot(o_ref[...], wo_ref[...], preferred_element_type=F32)
    y_ref[...] = _swiglu_residual(x, g_ref, wgu_ref, wd_ref)


def _sb_out_ffn(x, o, w_out, g, w_gate_up, w_down):
    t, d = x.shape
    return pl.pallas_call(
        _sb_out_ffn_body,
        out_shape=jax.ShapeDtypeStruct((t, d), F32),
        grid=(t // ROW_TILE,),
        in_specs=[_rows(d), _rows(d), _resident(w_out.shape), _resident((1, d)),
                  _resident(w_gate_up.shape), _resident(w_down.shape)],
        out_specs=_rows(d),
        compiler_params=_params(1),
        name="sb_out_ffn",
    )(x, o, w_out, g, w_gate_up, w_down)


def _gla_out_ffn_body(x_ref, o_ref, r_ref, ng_ref, wo_ref, g_ref, wgu_ref, wd_ref, y_ref):
    hv = ng_ref.shape[1]
    o = o_ref[...]
    r = r_ref[...].astype(F32)
    ng = ng_ref[...]
    parts = []
    for h in range(o.shape[1] // hv):
        sl = slice(h * hv, (h + 1) * hv)
        rh = r[:, sl]
        parts.append((_rmsnorm(o[:, sl], ng) * (rh * jax.nn.sigmoid(rh))).astype(BF16))
    gated = jnp.concatenate(parts, axis=-1)
    x = x_ref[...] + jnp.dot(gated, wo_ref[...], preferred_element_type=F32)
    y_ref[...] = _swiglu_residual(x, g_ref, wgu_ref, wd_ref)


def _gla_out_ffn(x, o, r, ng, w_out, g, w_gate_up, w_down):
    t, d = x.shape
    vd = o.shape[1]
    return pl.pallas_call(
        _gla_out_ffn_body,
        out_shape=jax.ShapeDtypeStruct((t, d), F32),
        grid=(t // ROW_TILE,),
        in_specs=[_rows(d), _rows(vd), _rows(vd), _resident(ng.shape), _resident(w_out.shape),
                  _resident((1, d)), _resident(w_gate_up.shape), _resident(w_down.shape)],
        out_specs=_rows(d),
        compiler_params=_params(1),
        name="gla_out_ffn",
    )(x, o, r, ng, w_out, g, w_gate_up, w_down)


def _sb_inproj_body(x_ref, g_ref, w_ref, q_ref, k_ref, v_ref):
    d = x_ref.shape[1]
    h = _rmsnorm(x_ref[...], g_ref[...]).astype(BF16)
    qkv = jnp.dot(h, w_ref[...], preferred_element_type=F32)
    q_ref[...] = qkv[:, :d].astype(BF16)
    k_ref[...] = qkv[:, d:2 * d].astype(BF16)
    v_ref[...] = qkv[:, 2 * d:].astype(BF16)


def _sb_inproj(x, g, w):
    t, d = x.shape
    out = jax.ShapeDtypeStruct((t, d), BF16)
    return pl.pallas_call(
        _sb_inproj_body,
        out_shape=(out, out, out),
        grid=(t // ROW_TILE,),
        in_specs=[_rows(d), _resident((1, d)), _resident((d, 3 * d))],
        out_specs=(_rows(d), _rows(d), _rows(d)),
        compiler_params=_params(1),
        name="sb_inproj",
    )(x, g, w)


def _softplus(z):
    return jnp.maximum(z, 0.0) + jnp.log(1.0 + jnp.exp(-jnp.abs(z)))


def _sb_attn_body(q_ref, k_ref, v_ref, gq_ref, gk_ref, o_ref,
                  qa_ref, qb_ref, kn_ref, acc_a_ref, acc_b_ref, rem_a_ref, rem_b_ref):
    seq = q_ref.shape[1]
    blk = SB_BLOCK
    nb = seq // blk
    lane = lax.broadcasted_iota(jnp.int32, (1, LANES), 1)
    first = lane < SB_HEAD_DIM

    def head_norm(t_ref, g_ref, scale):
        t = t_ref[0].astype(F32)
        sq = t * t
        ss_a = jnp.sum(jnp.where(first, sq, 0.0), axis=-1, keepdims=True)
        ss_b = jnp.sum(jnp.where(first, 0.0, sq), axis=-1, keepdims=True)
        inv = jnp.where(first,
                        lax.rsqrt(ss_a / SB_HEAD_DIM + NORM_EPS),
                        lax.rsqrt(ss_b / SB_HEAD_DIM + NORM_EPS))
        return (t * inv * g_ref[...] * scale).astype(BF16)

    qn = head_norm(q_ref, gq_ref, SB_HEAD_DIM ** -0.5)
    qa_ref[...] = jnp.where(first, qn, jnp.zeros_like(qn))
    qb_ref[...] = jnp.where(first, jnp.zeros_like(qn), qn)
    kn_ref[...] = head_norm(k_ref, gk_ref, 1.0)

    row = lax.broadcasted_iota(jnp.int32, (blk, blk), 0)
    col = lax.broadcasted_iota(jnp.int32, (blk, blk), 1)
    strict = col < row
    row2 = lax.broadcasted_iota(jnp.int32, (blk, 2 * blk), 0)
    col2 = lax.broadcasted_iota(jnp.int32, (blk, 2 * blk), 1)
    later_and_total = ((row2 > col2) | (col2 >= blk)).astype(BF16)

    def mask_first_block(x):
        masked = jnp.where(strict, x[:blk], 0.0)
        return masked if x.shape[0] == blk else jnp.concatenate([masked, x[blk:]], axis=0)

    def scores_stage(q_rows, kk, diagonal):
        z = lax.dot_general(q_rows, kk, _NT, preferred_element_type=F32)
        sp = _softplus(z)
        spm = mask_first_block(sp) if diagonal else sp
        return z - sp, spm.astype(BF16)

    def sums_stage(spm):
        return jnp.dot(spm, later_and_total, preferred_element_type=F32)

    def weights_stage(log_beta, sums, rem, vv, diagonal):
        w = jnp.exp(log_beta - sums[:, :blk] - rem)
        if diagonal:
            w = mask_first_block(w)
        return jnp.dot(w.astype(BF16), vv, preferred_element_type=F32)

    def block_step(q_rows, kk, vv, rem):
        log_beta, spm = scores_stage(q_rows, kk, False)
        sums = sums_stage(spm)
        return weights_stage(log_beta, sums, rem, vv, False), sums[:, blk:]

    heads = ((qa_ref, acc_a_ref, rem_a_ref), (qb_ref, acc_b_ref, rem_b_ref))

    carried = [[], []]
    least_rem = None
    key_blocks = list(range(nb - 1, -1, -1))
    for g0 in range(0, nb, SB_GROUP):
        units = [(kb, h) for kb in key_blocks[g0:g0 + SB_GROUP] for h in range(len(heads))]
        rows_of = lambda kb: slice(kb * blk, (kb + min(SB_BAND, nb - kb)) * blk)
        keys_of = lambda kb: slice(kb * blk, (kb + 1) * blk)
        staged = [scores_stage(heads[h][0][rows_of(kb), :], kn_ref[keys_of(kb), :], True)
                  for kb, h in units]
        sums = [sums_stage(spm) for _, spm in staged]
        for (kb, h), (log_beta, _), unit_sums in zip(units, staged, sums):
            _, acc_ref, rem_ref = heads[h]
            m = min(SB_BAND, nb - kb)
            r0 = kb * blk
            rems = [jnp.zeros((blk, blk), F32)] + carried[h][:m - 1]
            out = weights_stage(log_beta, unit_sums, jnp.concatenate(rems, axis=0),
                                v_ref[0, keys_of(kb), :], True)
            acc_ref[r0:r0 + blk, :] = out[:blk]
            if m > 1:
                acc_ref[r0 + blk:r0 + m * blk, :] += out[blk:]
            rems = [rems[j] + unit_sums[j * blk:(j + 1) * blk, blk:] for j in range(m)]
            if m == SB_BAND and kb > 0:
                rem_ref[r0 + (m - 1) * blk:r0 + m * blk, :] = rems[m - 1]
                least_rem = rems[m - 1] if least_rem is None else jnp.minimum(least_rem, rems[m - 1])
            carried[h] = rems[:SB_BAND - 1]

    def need_more(rem_a, rem_b):
        return (jnp.minimum(jnp.min(rem_a), jnp.min(rem_b)) < SB_ZERO_WEIGHT).astype(jnp.int32)

    def continuation(i, _):
        qs = pl.multiple_of(i * blk, blk)

        def body(c):
            kb, _, rem_a, rem_b = c
            ks = pl.multiple_of(kb * blk, blk)
            kk = kn_ref[pl.ds(ks, blk), :]
            vv = v_ref[0, pl.ds(ks, blk), :]
            new_rems = []
            for (qsel_ref, acc_ref, _), rem in zip(heads, (rem_a, rem_b)):
                out, tot = block_step(qsel_ref[pl.ds(qs, blk), :], kk, vv, rem)
                acc_ref[pl.ds(qs, blk), :] += out
                new_rems.append(rem + tot)
            return kb - 1, need_more(*new_rems), new_rems[0], new_rems[1]

        rem_a = rem_a_ref[pl.ds(qs, blk), :]
        rem_b = rem_b_ref[pl.ds(qs, blk), :]
        lax.while_loop(lambda c: (c[0] >= 0) & (c[1] > 0), body,
                       (i - SB_BAND, need_more(rem_a, rem_b), rem_a, rem_b))
        return 0

    if least_rem is not None:
        @pl.when(jnp.min(least_rem) < SB_ZERO_WEIGHT)
        def _():
            lax.fori_loop(SB_BAND, nb, continuation, 0)

    o_ref[0] = jnp.where(first, acc_a_ref[...], acc_b_ref[...]).astype(BF16)


def _sb_attn(q, k, v, gq, gk):
    b, s, d = q.shape
    pairs = d // LANES
    blockspec = pl.BlockSpec((1, s, LANES), lambda i, j: (i, 0, j))
    gain = pl.BlockSpec((1, LANES), lambda i, j: (0, 0))
    return pl.pallas_call(
        _sb_attn_body,
        out_shape=jax.ShapeDtypeStruct((b, s, d), BF16),
        grid=(b, pairs),
        in_specs=[blockspec, blockspec, blockspec, gain, gain],
        out_specs=blockspec,
        scratch_shapes=[pltpu.VMEM((s, LANES), BF16)] * 3 + [pltpu.VMEM((s, LANES), F32)] * 4,
        compiler_params=_params(2),
        name="sb_attn",
    )(q, k, v, gq, gk)


def _sb_mixer(x, batch, attn_g, w_in, gq, gk):
    t, d = x.shape
    q, k, v = _sb_inproj(x, attn_g, w_in)
    shape3 = (batch, t // batch, d)
    reps = LANES // SB_HEAD_DIM
    o = _sb_attn(q.reshape(shape3), k.reshape(shape3), v.reshape(shape3),
                 jnp.tile(gq, reps)[None, :], jnp.tile(gk, reps)[None, :])
    return o.reshape(t, d)


def _gla_inproj_body(x_ref, g_ref, w_ref, wg1_ref, wg2_ref, bg_ref,
                     q_ref, k_ref, v_ref, r_ref, la_ref, *, kd, vd, q_scale):
    h = _rmsnorm(x_ref[...], g_ref[...]).astype(BF16)
    proj = jnp.dot(h, w_ref[...], preferred_element_type=F32)
    q_ref[...] = (proj[:, :kd] * q_scale).astype(BF16)
    k_ref[...] = proj[:, kd:2 * kd].astype(BF16)
    v_ref[...] = proj[:, 2 * kd:2 * kd + vd].astype(BF16)
    r_ref[...] = proj[:, 2 * kd + vd:].astype(BF16)
    low_rank = jnp.dot(h, wg1_ref[...], preferred_element_type=F32).astype(BF16)
    gate = jnp.dot(low_rank, wg2_ref[...], preferred_element_type=F32) + bg_ref[...]
    softplus_neg = jnp.maximum(-gate, 0.0) + jnp.log(1.0 + jnp.exp(-jnp.abs(gate)))
    la_ref[...] = -softplus_neg / GLA_TAU


def _gla_inproj(x, g, w, wg1, wg2, bg, kd, vd):
    t, d = x.shape
    body = functools.partial(_gla_inproj_body, kd=kd, vd=vd,
                             q_scale=(kd // GLA_HEADS) ** -0.5)
    return pl.pallas_call(
        body,
        out_shape=(jax.ShapeDtypeStruct((t, kd), BF16), jax.ShapeDtypeStruct((t, kd), BF16),
                   jax.ShapeDtypeStruct((t, vd), BF16), jax.ShapeDtypeStruct((t, vd), BF16),
                   jax.ShapeDtypeStruct((t, kd), F32)),
        grid=(t // ROW_TILE,),
        in_specs=[_rows(d), _resident((1, d)), _resident(w.shape), _resident(wg1.shape),
                  _resident(wg2.shape), _resident((1, kd))],
        out_specs=(_rows(kd), _rows(kd), _rows(vd), _rows(vd), _rows(kd)),
        compiler_params=_params(1),
        name="gla_inproj",
    )(x, g, w, wg1, wg2, bg)


def _gla_scan_body(q_ref, k_ref, la_ref, v_ref, o_ref):
    seq = q_ref.shape[1]
    c = GLA_CHUNK
    row = lax.broadcasted_iota(jnp.int32, (c, c), 0)
    col = lax.broadcasted_iota(jnp.int32, (c, c), 1)
    causal = col <= row
    prefix = causal.astype(BF16)
    chunks = [slice(n * c, (n + 1) * c) for n in range(seq // c)]
    hk = q_ref.shape[2]

    cums = []
    for rows in chunks:
        la_hi, la_lo = _split_bf16(la_ref[0, rows, :])
        both = jnp.dot(prefix, jnp.concatenate([la_hi, la_lo], axis=1), preferred_element_type=F32)
        cums.append(both[:, :hk] + both[:, hk:])
    qds, k_ends, decays, intra = [], [], [], []
    for rows, cum in zip(chunks, cums):
        qf = q_ref[0, rows, :].astype(F32)
        kf = k_ref[0, rows, :].astype(F32)
        last = cum[c - 1:c, :]
        qd = (qf * jnp.exp(cum)).astype(BF16)
        kd = (kf * jnp.exp(-cum)).astype(BF16)
        qds.append(qd)
        k_ends.append((kf * jnp.exp(last - cum)).astype(BF16))
        decays.append(jnp.exp(last))
        scores = lax.dot_general(qd, kd, _NT, preferred_element_type=F32)
        scores = jnp.where(causal, scores, 0.0).astype(BF16)
        intra.append(jnp.dot(scores, v_ref[0, rows, :], preferred_element_type=F32))
    updates = [lax.dot_general(v_ref[0, rows, :], k_end, _TN, preferred_element_type=F32)
               for rows, k_end in zip(chunks, k_ends)]
    state = None
    for n, rows in enumerate(chunks):
        out = intra[n]
        if state is not None:
            out = out + lax.dot_general(qds[n], state.astype(BF16), _NT, preferred_element_type=F32)
        o_ref[0, rows, :] = out
        state = updates[n] if state is None else state * decays[n] + updates[n]


def _gla_scan(q, k, la, v):
    b, s, kd = q.shape
    vd = v.shape[2]
    hk = kd // GLA_HEADS
    hv = vd // GLA_HEADS
    key_spec = pl.BlockSpec((1, s, hk), lambda i, j: (i, 0, j))
    val_spec = pl.BlockSpec((1, s, hv), lambda i, j: (i, 0, j))
    return pl.pallas_call(
        _gla_scan_body,
        out_shape=jax.ShapeDtypeStruct((b, s, vd), F32),
        grid=(b, GLA_HEADS),
        in_specs=[key_spec, key_spec, key_spec, val_spec],
        out_specs=val_spec,
        compiler_params=_params(2),
        name="gla_scan",
    )(q, k, la, v)


def _gla_mixer(x, batch, attn_g, w_in, w_gate2, b_gate, vd):
    t, d = x.shape
    kd = w_gate2.shape[1]
    main = 2 * kd + 2 * vd
    wg1 = jnp.pad(w_in[:, main:], ((0, 0), (0, LANES - GLA_GATE_RANK)))
    wg2 = jnp.pad(w_gate2, ((0, LANES - GLA_GATE_RANK), (0, 0)))
    q, k, v, r, la = _gla_inproj(x, attn_g, w_in[:, :main], wg1, wg2, b_gate[None, :], kd, vd)
    s = t // batch
    o = _gla_scan(q.reshape(batch, s, kd), k.reshape(batch, s, kd), la.reshape(batch, s, kd),
                  v.reshape(batch, s, vd))
    return o.reshape(t, vd), r


def kernel(x, attn_norm_g, ffn_norm_g, gla_w_in, gla_w_gate2, gla_b_gate, gla_norm_g, gla_w_out,
           sb_w_in, sb_q_norm_g, sb_k_norm_g, sb_w_out, ffn_w_gate_up, ffn_w_down):
    batch, seq, d = x.shape
    depth = attn_norm_g.shape[0]
    assert (batch * seq) % ROW_TILE == 0 and seq % GLA_CHUNK == 0 and seq % SB_BLOCK == 0
    bf = lambda w: w.astype(BF16)
    xt = x.reshape(batch * seq, d)
    for i in range(depth):
        j = i // 2
        attn_g = attn_norm_g[i][None, :]
        ffn = (ffn_norm_g[i][None, :], bf(ffn_w_gate_up[i]), bf(ffn_w_down[i]))
        if i % 2 == 0:
            o, r = _gla_mixer(xt, batch, attn_g, bf(gla_w_in[j]), bf(gla_w_gate2[j]),
                              gla_b_gate[j], gla_w_out.shape[1])
            xt = _gla_out_ffn(xt, o, r, gla_norm_g[j][None, :], bf(gla_w_out[j]), *ffn)
        else:
            o = _sb_mixer(xt, batch, attn_g, bf(sb_w_in[j]), sb_q_norm_g[j], sb_k_norm_g[j])
            xt = _sb_out_ffn(xt, o, bf(sb_w_out[j]), *ffn)
    return xt.reshape(batch, seq, d)
```

```python
import functools

import jax
import jax.numpy as jnp
from jax import lax
from jax.experimental import pallas as pl
from jax.experimental.pallas import tpu as pltpu

F32 = jnp.float32
BF16 = jnp.bfloat16

NORM_EPS = 1e-6
GLA_HEADS = 4
GLA_TAU = 16.0
GLA_GATE_RANK = 16
SB_HEAD_DIM = 64

LANES = 128
ROW_TILE = 512
GLA_CHUNK = 128
GLA_SAFE_DECAY = 60.0
SB_BLOCK = 128
SB_PASS1_ROWS = (128, 128, 48)
SB_GROUP = 4
BF16_SUBLANES = 16
SB_ZERO_WEIGHT = 150.0
LOG2E = 1.4426950408889634
VMEM_LIMIT = 56 * 1024 * 1024

_NT = (((1,), (1,)), ((), ()))
_TN = (((0,), (0,)), ((), ()))


def _rmsnorm(xf, g):
    return xf * lax.rsqrt(jnp.mean(xf * xf, axis=-1, keepdims=True) + NORM_EPS) * g


def _split_bf16(x):
    hi = x.astype(BF16)
    lo = (x - hi.astype(F32)).astype(BF16)
    return hi, lo


def _params(n_parallel):
    return pltpu.CompilerParams(
        dimension_semantics=("parallel",) * n_parallel,
        vmem_limit_bytes=VMEM_LIMIT,
    )


def _resident(shape):
    return pl.BlockSpec(shape, lambda *_: (0,) * len(shape), pipeline_mode=pl.Buffered(1))


def _rows(width):
    return pl.BlockSpec((ROW_TILE, width), lambda i: (i, 0))


def _swiglu_residual(x, g_ref, wgu_ref, wd_ref):
    d_ff = wd_ref.shape[0]
    h = _rmsnorm(x, g_ref[...]).astype(BF16)
    gu = jnp.dot(h, wgu_ref[...], preferred_element_type=F32)
    gate = gu[:, :d_ff]
    up = gu[:, d_ff:]
    act = (gate * jax.nn.sigmoid(gate) * up).astype(BF16)
    return x + jnp.dot(act, wd_ref[...], preferred_element_type=F32)


def _sb_out_ffn_body(x_ref, o_ref, wo_ref, g_ref, wgu_ref, wd_ref, y_ref):
    x = x_ref[...] + jnp.dot(o_ref[...], wo_ref[...], preferred_element_type=F32)
    y_ref[...] = _swiglu_residual(x, g_ref, wgu_ref, wd_ref)


def _sb_out_ffn(x, o, w_out, g, w_gate_up, w_down):
    t, d = x.shape
    return pl.pallas_call(
        _sb_out_ffn_body,
        out_shape=jax.ShapeDtypeStruct((t, d), F32),
        grid=(t // ROW_TILE,),
        in_specs=[_rows(d), _rows(d), _resident(w_out.shape), _resident((1, d)),
                  _resident(w_gate_up.shape), _resident(w_down.shape)],
        out_specs=_rows(d),
        compiler_params=_params(1),
        name="sb_out_ffn",
    )(x, o, w_out, g, w_gate_up, w_down)


def _gla_out_ffn_body(x_ref, o_ref, r_ref, ng_ref, wo_ref, g_ref, wgu_ref, wd_ref, y_ref):
    hv = ng_ref.shape[1]
    o = o_ref[...]
    r = r_ref[...].astype(F32)
    ng = ng_ref[...]
    parts = []
    for h in range(o.shape[1] // hv):
        sl = slice(h * hv, (h + 1) * hv)
        rh = r[:, sl]
        parts.append((_rmsnorm(o[:, sl], ng) * (rh * jax.nn.sigmoid(rh))).astype(BF16))
    gated = jnp.concatenate(parts, axis=-1)
    x = x_ref[...] + jnp.dot(gated, wo_ref[...], preferred_element_type=F32)
    y_ref[...] = _swiglu_residual(x, g_ref, wgu_ref, wd_ref)


def _gla_out_ffn(x, o, r, ng, w_out, g, w_gate_up, w_down):
    t, d = x.shape
    vd = o.shape[1]
    return pl.pallas_call(
        _gla_out_ffn_body,
        out_shape=jax.ShapeDtypeStruct((t, d), F32),
        grid=(t // ROW_TILE,),
        in_specs=[_rows(d), _rows(vd), _rows(vd), _resident(ng.shape), _resident(w_out.shape),
                  _resident((1, d)), _resident(w_gate_up.shape), _resident(w_down.shape)],
        out_specs=_rows(d),
        compiler_params=_params(1),
        name="gla_out_ffn",
    )(x, o, r, ng, w_out, g, w_gate_up, w_down)


def _sb_inproj_body(x_ref, g_ref, w_ref, gq_ref, gk_ref, q_ref, k_ref, v_ref):
    d = x_ref.shape[1]
    h = _rmsnorm(x_ref[...], g_ref[...]).astype(BF16)
    qkv = jnp.dot(h, w_ref[...], preferred_element_type=F32)
    first = lax.broadcasted_iota(jnp.int32, (1, LANES), 1) < SB_HEAD_DIM

    def head_norm(t, gain):
        sq = t * t
        ss_a = jnp.sum(jnp.where(first, sq, 0.0), axis=-1, keepdims=True)
        ss_b = jnp.sum(jnp.where(first, 0.0, sq), axis=-1, keepdims=True)
        inv = jnp.where(first,
                        lax.rsqrt(ss_a / SB_HEAD_DIM + NORM_EPS),
                        lax.rsqrt(ss_b / SB_HEAD_DIM + NORM_EPS))
        return (t * inv * gain).astype(BF16)

    gq = gq_ref[...] * (SB_HEAD_DIM ** -0.5 * LOG2E)
    gk = gk_ref[...]
    for c in range(d // LANES):
        cols = slice(c * LANES, (c + 1) * LANES)
        q_ref[:, cols] = head_norm(qkv[:, cols], gq)
        k_ref[:, cols] = head_norm(qkv[:, d + c * LANES:d + (c + 1) * LANES], gk)
    v_ref[...] = qkv[:, 2 * d:].astype(BF16)


def _sb_inproj(x, g, w, gq, gk):
    t, d = x.shape
    out = jax.ShapeDtypeStruct((t, d), BF16)
    return pl.pallas_call(
        _sb_inproj_body,
        out_shape=(out, out, out),
        grid=(t // ROW_TILE,),
        in_specs=[_rows(d), _resident((1, d)), _resident((d, 3 * d)),
                  _resident((1, LANES)), _resident((1, LANES))],
        out_specs=(_rows(d), _rows(d), _rows(d)),
        compiler_params=_params(1),
        name="sb_inproj",
    )(x, g, w, gq, gk)


def _softplus_bits(z):
    return jnp.maximum(z, 0.0) + jnp.log(1.0 + jnp.exp2(-jnp.abs(z))) * LOG2E


def _sb_attn_body(q_ref, k_ref, v_ref, o_ref,
                  ka_ref, kb_ref, va_ref, vb_ref, acc_ref, rem_a_ref, rem_b_ref):
    seq = q_ref.shape[1]
    blk = SB_BLOCK
    nb = seq // blk
    first = lax.broadcasted_iota(jnp.int32, (1, LANES), 1) < SB_HEAD_DIM
    kk = k_ref[0]
    vv = v_ref[0]
    zero = jnp.zeros_like(kk)
    ka_ref[...] = jnp.where(first, kk, zero)
    kb_ref[...] = jnp.where(first, zero, kk)
    va_ref[...] = jnp.where(first, vv, zero)
    vb_ref[...] = jnp.where(first, zero, vv)

    row2 = lax.broadcasted_iota(jnp.int32, (blk, 2 * blk), 0)
    col2 = lax.broadcasted_iota(jnp.int32, (blk, 2 * blk), 1)
    strict = (col2 & (blk - 1)) < row2
    later_and_total = ((row2 > col2) | (col2 >= blk)).astype(BF16)
    rem_refs = (rem_a_ref, rem_b_ref)
    n_heads = len(rem_refs)

    def keys_values(keys):
        return (jnp.concatenate([ka_ref[keys, :], kb_ref[keys, :]], axis=0),
                jnp.concatenate([va_ref[keys, :], vb_ref[keys, :]], axis=0))

    def mask_first_block(x):
        masked = jnp.where(strict, x[:blk], 0.0)
        return masked if x.shape[0] == blk else jnp.concatenate([masked, x[blk:]], axis=0)

    def scores_stage(q_rows, k2, diagonal):
        z = lax.dot_general(q_rows, k2, _NT, preferred_element_type=F32)
        sp = _softplus_bits(z)
        spm = mask_first_block(sp) if diagonal else sp
        return z - sp, spm.astype(BF16)

    def sums_stage(spm):
        return [jnp.dot(spm[:, h * blk:(h + 1) * blk], later_and_total, preferred_element_type=F32)
                for h in range(n_heads)]

    def weights_stage(log_beta, sums, rems, v2, diagonal):
        later = jnp.concatenate([s[:, :blk] for s in sums], axis=1)
        w = jnp.exp2(log_beta - later - jnp.concatenate(rems, axis=1))
        if diagonal:
            w = mask_first_block(w)
        return jnp.dot(w.astype(BF16), v2, preferred_element_type=F32)

    def block_step(q_rows, k2, v2, rems):
        log_beta, spm = scores_stage(q_rows, k2, False)
        sums = sums_stage(spm)
        return weights_stage(log_beta, sums, rems, v2, False), [s[:, blk:] for s in sums]

    span_lo = SB_PASS1_ROWS[1:] + (0,)
    carried = [{}, {}]
    least_rem = {}
    key_blocks = list(range(nb - 1, -1, -1))
    keys_of = lambda kb: slice(kb * blk, (kb + 1) * blk)

    def rows_of(kb):
        offsets = [j for j in range(len(SB_PASS1_ROWS)) if kb + j < nb]
        return slice(kb * blk, (kb + offsets[-1]) * blk + SB_PASS1_ROWS[offsets[-1]]), offsets

    for g0 in range(0, nb, SB_GROUP):
        group = key_blocks[g0:g0 + SB_GROUP]
        operands = [keys_values(keys_of(kb)) for kb in group]
        staged = [scores_stage(q_ref[0,rows_of(kb)[0], :], k2, True)
                  for kb, (k2, _) in zip(group, operands)]
        sums = [sums_stage(spm) for _, spm in staged]
        for kb, (_, v2), (log_beta, _), unit_sums in zip(group, operands, staged, sums):
            rows, offsets = rows_of(kb)
            rems = [jnp.concatenate([jnp.zeros((blk, blk), F32)] + [carried[h][j] for j in offsets[1:]],
                                    axis=0) for h in range(n_heads)]
            out = weights_stage(log_beta, unit_sums, rems, v2, True)
            acc_ref[keys_of(kb), :] = out[:blk]
            if len(offsets) > 1:
                acc_ref[rows.start + blk:rows.stop, :] += out[blk:]
            for h in range(n_heads):
                after = rems[h] + unit_sums[h][:, blk:]
                for j in offsets:
                    rem = after[j * blk:j * blk + SB_PASS1_ROWS[j]]
                    if j + 1 < len(SB_PASS1_ROWS):
                        carried[h][j + 1] = rem[:SB_PASS1_ROWS[j + 1]]
                    if kb > 0 and span_lo[j] < SB_PASS1_ROWS[j]:
                        done = rem[span_lo[j]:]
                        rem_refs[h][(kb + j) * blk + span_lo[j]:(kb + j) * blk + SB_PASS1_ROWS[j], :] = done
                        least_rem[j] = jnp.minimum(least_rem[j], done) if j in least_rem else done

    def need_more(rems):
        least = functools.reduce(jnp.minimum, [jnp.min(rem) for rem in rems])
        return (least < SB_ZERO_WEIGHT).astype(jnp.int32)

    def continue_rows(j):
        n_rows = SB_PASS1_ROWS[j] - span_lo[j]

        def query_block(i, _):
            q_rows = pl.ds(pl.multiple_of(i * blk + span_lo[j], BF16_SUBLANES), n_rows)

            def body(c):
                kb, _, *rems = c
                k2, v2 = keys_values(pl.ds(pl.multiple_of(kb * blk, blk), blk))
                out, totals = block_step(q_ref[0,q_rows, :], k2, v2, rems)
                acc_ref[q_rows, :] += out
                rems = [rem + tot for rem, tot in zip(rems, totals)]
                return (kb - 1, need_more(rems), *rems)

            rems = [rem_ref[q_rows, :] for rem_ref in rem_refs]
            lax.while_loop(lambda c: (c[0] >= 0) & (c[1] > 0), body,
                           (i - j - 1, need_more(rems), *rems))
            return 0

        lax.fori_loop(j + 1, nb, query_block, 0)

    for j, least in least_rem.items():
        @pl.when(jnp.min(least) < SB_ZERO_WEIGHT)
        def _():
            continue_rows(j)

    o_ref[0] = acc_ref[...].astype(BF16)


def _sb_attn(q, k, v):
    b, s, d = q.shape
    pairs = d // LANES
    blockspec = pl.BlockSpec((1, s, LANES), lambda i, j: (i, 0, j))
    return pl.pallas_call(
        _sb_attn_body,
        out_shape=jax.ShapeDtypeStruct((b, s, d), BF16),
        grid=(b, pairs),
        in_specs=[blockspec, blockspec, blockspec],
        out_specs=blockspec,
        scratch_shapes=[pltpu.VMEM((s, LANES), BF16)] * 4 + [pltpu.VMEM((s, LANES), F32)] * 3,
        compiler_params=_params(2),
        name="sb_attn",
    )(q, k, v)


def _sb_mixer(x, batch, attn_g, w_in, gq, gk):
    t, d = x.shape
    reps = LANES // SB_HEAD_DIM
    q, k, v = _sb_inproj(x, attn_g, w_in, jnp.tile(gq, reps)[None, :], jnp.tile(gk, reps)[None, :])
    shape3 = (batch, t // batch, d)
    o = _sb_attn(q.reshape(shape3), k.reshape(shape3), v.reshape(shape3))
    return o.reshape(t, d)


def _gla_inproj_body(x_ref, g_ref, w_ref, wg1_ref, wg2_ref, bg_ref,
                     q_ref, k_ref, v_ref, r_ref, la_ref, *, kd, vd, q_scale):
    h = _rmsnorm(x_ref[...], g_ref[...]).astype(BF16)
    proj = jnp.dot(h, w_ref[...], preferred_element_type=F32)
    q_ref[...] = (proj[:, :kd] * q_scale).astype(BF16)
    k_ref[...] = proj[:, kd:2 * kd].astype(BF16)
    v_ref[...] = proj[:, 2 * kd:2 * kd + vd].astype(BF16)
    r_ref[...] = proj[:, 2 * kd + vd:].astype(BF16)
    low_rank = jnp.dot(h, wg1_ref[...], preferred_element_type=F32).astype(BF16)
    gate = jnp.dot(low_rank, wg2_ref[...], preferred_element_type=F32) + bg_ref[...]
    softplus_neg = jnp.maximum(-gate, 0.0) + jnp.log(1.0 + jnp.exp(-jnp.abs(gate)))
    la_ref[...] = -softplus_neg / GLA_TAU


def _gla_inproj(x, g, w, wg1, wg2, bg, kd, vd):
    t, d = x.shape
    body = functools.partial(_gla_inproj_body, kd=kd, vd=vd,
                             q_scale=(kd // GLA_HEADS) ** -0.5)
    return pl.pallas_call(
        body,
        out_shape=(jax.ShapeDtypeStruct((t, kd), BF16), jax.ShapeDtypeStruct((t, kd), BF16),
                   jax.ShapeDtypeStruct((t, vd), BF16), jax.ShapeDtypeStruct((t, vd), BF16),
                   jax.ShapeDtypeStruct((t, kd), F32)),
        grid=(t // ROW_TILE,),
        in_specs=[_rows(d), _resident((1, d)), _resident(w.shape), _resident(wg1.shape),
                  _resident(wg2.shape), _resident((1, kd))],
        out_specs=(_rows(kd), _rows(kd), _rows(vd), _rows(vd), _rows(kd)),
        compiler_params=_params(1),
        name="gla_inproj",
    )(x, g, w, wg1, wg2, bg)


def _gla_scan_body(q_ref, k_ref, la_ref, v_ref, o_ref, rowbuf_ref):
    seq = q_ref.shape[1]
    c = GLA_CHUNK
    row = lax.broadcasted_iota(jnp.int32, (c, c), 0)
    col = lax.broadcasted_iota(jnp.int32, (c, c), 1)
    causal = col <= row
    prefix = causal.astype(BF16)
    chunks = [slice(n * c, (n + 1) * c) for n in range(seq // c)]
    hk = q_ref.shape[2]

    def chunk_cum(la):
        la_hi, la_lo = _split_bf16(la)
        both = jnp.dot(prefix, jnp.concatenate([la_hi, la_lo], axis=1), preferred_element_type=F32)
        return both[:, :hk] + both[:, hk:]

    def factored_scores(qd, kf, cum):
        kd = (kf * jnp.exp(jnp.minimum(-cum, GLA_SAFE_DECAY))).astype(BF16)
        scores = lax.dot_general(qd, kd, _NT, preferred_element_type=F32)
        return jnp.where(causal, scores, 0.0).astype(BF16)

    cums = [chunk_cum(la_ref[0, rows, :]) for rows in chunks]
    qds, k_ends, decays, intra = [], [], [], []
    lowest = cums[0][c - 1:c, :]
    for rows, cum in zip(chunks, cums):
        qf = q_ref[0, rows, :].astype(F32)
        kf = k_ref[0, rows, :].astype(F32)
        last = cum[c - 1:c, :]
        lowest = jnp.minimum(lowest, last)
        qd = (qf * jnp.exp(cum)).astype(BF16)
        qds.append(qd)
        k_ends.append((kf * jnp.exp(last - cum)).astype(BF16))
        decays.append(jnp.exp(last))
        intra.append(jnp.dot(factored_scores(qd, kf, cum), v_ref[0, rows, :],
                             preferred_element_type=F32))
    updates = [lax.dot_general(v_ref[0, rows, :], k_end, _TN, preferred_element_type=F32)
               for rows, k_end in zip(chunks, k_ends)]
    state = None
    for n, rows in enumerate(chunks):
        out = intra[n]
        if state is not None:
            out = out + lax.dot_general(qds[n], state.astype(BF16), _NT, preferred_element_type=F32)
        o_ref[0, rows, :] = out
        state = updates[n] if state is None else state * decays[n] + updates[n]

    def redo_chunk(n, _):
        rows = pl.ds(pl.multiple_of(n * c, c), c)
        cum = chunk_cum(la_ref[0, rows, :])

        @pl.when(jnp.min(cum[c - 1:c, :]) < -GLA_SAFE_DECAY)
        def _():
            qf = q_ref[0, rows, :].astype(F32)
            kf = k_ref[0, rows, :].astype(F32)
            vv = v_ref[0, rows, :]
            rowbuf_ref[0] = kf
            rowbuf_ref[1] = cum

            def key_column(s, scores):
                k_s = rowbuf_ref[0, pl.ds(s, 1), :]
                cum_s = rowbuf_ref[1, pl.ds(s, 1), :]
                terms = qf * jnp.exp(jnp.minimum(cum - cum_s, 0.0)) * k_s
                column = jnp.sum(terms, axis=-1, keepdims=True)
                return jnp.where((col == s) & (row >= s), column, scores)

            exact = lax.fori_loop(0, c, key_column, jnp.zeros((c, c), F32)).astype(BF16)
            factored = factored_scores((qf * jnp.exp(cum)).astype(BF16), kf, cum)
            o_ref[0, rows, :] += (jnp.dot(exact, vv, preferred_element_type=F32)
                                  - jnp.dot(factored, vv, preferred_element_type=F32))
        return 0

    @pl.when(jnp.min(lowest) < -GLA_SAFE_DECAY)
    def _():
        lax.fori_loop(0, len(chunks), redo_chunk, 0)


def _gla_scan(q, k, la, v):
    b, s, kd = q.shape
    vd = v.shape[2]
    hk = kd // GLA_HEADS
    hv = vd // GLA_HEADS
    key_spec = pl.BlockSpec((1, s, hk), lambda i, j: (i, 0, j))
    val_spec = pl.BlockSpec((1, s, hv), lambda i, j: (i, 0, j))
    return pl.pallas_call(
        _gla_scan_body,
        out_shape=jax.ShapeDtypeStruct((b, s, vd), F32),
        grid=(b, GLA_HEADS),
        in_specs=[key_spec, key_spec, key_spec, val_spec],
        out_specs=val_spec,
        scratch_shapes=[pltpu.VMEM((2, GLA_CHUNK, hk), F32)],
        compiler_params=_params(2),
        name="gla_scan",
    )(q, k, la, v)


def _gla_mixer(x, batch, attn_g, w_in, w_gate2, b_gate, vd):
    t, d = x.shape
    kd = w_gate2.shape[1]
    main = 2 * kd + 2 * vd
    wg1 = jnp.pad(w_in[:, main:], ((0, 0), (0, LANES - GLA_GATE_RANK)))
    wg2 = jnp.pad(w_gate2, ((0, LANES - GLA_GATE_RANK), (0, 0)))
    q, k, v, r, la = _gla_inproj(x, attn_g, w_in[:, :main], wg1, wg2, b_gate[None, :], kd, vd)
    s = t // batch
    o = _gla_scan(q.reshape(batch, s, kd), k.reshape(batch, s, kd), la.reshape(batch, s, kd),
                  v.reshape(batch, s, vd))
    return o.reshape(t, vd), r


def kernel(x, attn_norm_g, ffn_norm_g, gla_w_in, gla_w_gate2, gla_b_gate, gla_norm_g, gla_w_out,
           sb_w_in, sb_q_norm_g, sb_k_norm_g, sb_w_out, ffn_w_gate_up, ffn_w_down):
    batch, seq, d = x.shape
    depth = attn_norm_g.shape[0]
    assert (batch * seq) % ROW_TILE == 0 and seq % GLA_CHUNK == 0 and seq % SB_BLOCK == 0
    bf = lambda w: w.astype(BF16)
    xt = x.reshape(batch * seq, d)
    for i in range(depth):
        j = i // 2
        attn_g = attn_norm_g[i][None, :]
        ffn = (ffn_norm_g[i][None, :], bf(ffn_w_gate_up[i]), bf(ffn_w_down[i]))
        if i % 2 == 0:
            o, r = _gla_mixer(xt, batch, attn_g, bf(gla_w_in[j]), bf(gla_w_gate2[j]),
                              gla_b_gate[j], gla_w_out.shape[1])
            xt = _gla_out_ffn(xt, o, r, gla_norm_g[j][None, :], bf(gla_w_out[j]), *ffn)
        else:
            o = _sb_mixer(xt, batch, attn_g, bf(sb_w_in[j]), sb_q_norm_g[j], sb_k_norm_g[j])
            xt = _sb_out_ffn(xt, o, bf(sb_w_out[j]), *ffn)
    return xt.reshape(batch, seq, d)
```

```python
import functools

import jax
import jax.numpy as jnp
from jax import lax
from jax.experimental import pallas as pl
from jax.experimental.pallas import tpu as pltpu

F32 = jnp.float32
BF16 = jnp.bfloat16

NORM_EPS = 1e-6
GLA_HEADS = 4
GLA_TAU = 16.0
GLA_GATE_RANK = 16
SB_HEAD_DIM = 64

LANES = 128
ROW_TILE = 512
INPROJ_ROW_TILE = 1024
GLA_CHUNK = 128
GLA_HEADS_PER_STEP = 2
GLA_SAFE_DECAY = 60.0
SB_BLOCK = LANES // 2
SB_PASS1_ROWS = (64, 64, 64, 32)
SB_GROUP = 4
BF16_SUBLANES = 16
SB_ZERO_WEIGHT = 150.0
LOG2E = 1.4426950408889634
VMEM_LIMIT = 56 * 1024 * 1024

_NT = (((1,), (1,)), ((), ()))
_TN = (((0,), (0,)), ((), ()))


def _rmsnorm(xf, g):
    return xf * lax.rsqrt(jnp.mean(xf * xf, axis=-1, keepdims=True) + NORM_EPS) * g


def _split_bf16(x):
    hi = x.astype(BF16)
    lo = (x - hi.astype(F32)).astype(BF16)
    return hi, lo


def _params(n_parallel):
    return pltpu.CompilerParams(
        dimension_semantics=("parallel",) * n_parallel,
        vmem_limit_bytes=VMEM_LIMIT,
    )


def _resident(shape):
    return pl.BlockSpec(shape, lambda *_: (0,) * len(shape), pipeline_mode=pl.Buffered(1))


def _rows(width, tile=ROW_TILE):
    return pl.BlockSpec((tile, width), lambda i: (i, 0))


def _swiglu_residual(x, g_ref, wgu_ref, wd_ref):
    d_ff = wd_ref.shape[0]
    h = _rmsnorm(x, g_ref[...]).astype(BF16)
    gu = jnp.dot(h, wgu_ref[...], preferred_element_type=F32)
    gate = gu[:, :d_ff]
    up = gu[:, d_ff:]
    act = (gate * jax.nn.sigmoid(gate) * up).astype(BF16)
    return x + jnp.dot(act, wd_ref[...], preferred_element_type=F32)


def _sb_out_ffn_body(x_ref, o_ref, wo_ref, g_ref, wgu_ref, wd_ref, y_ref):
    x = x_ref[...] + jnp.dot(o_ref[...], wo_ref[...], preferred_element_type=F32)
    y_ref[...] = _swiglu_residual(x, g_ref, wgu_ref, wd_ref)


def _sb_out_ffn(x, o, w_out, g, w_gate_up, w_down):
    t, d = x.shape
    return pl.pallas_call(
        _sb_out_ffn_body,
        out_shape=jax.ShapeDtypeStruct((t, d), F32),
        grid=(t // ROW_TILE,),
        in_specs=[_rows(d), _rows(d), _resident(w_out.shape), _resident((1, d)),
                  _resident(w_gate_up.shape), _resident(w_down.shape)],
        out_specs=_rows(d),
        compiler_params=_params(1),
        name="sb_out_ffn",
    )(x, o, w_out, g, w_gate_up, w_down)


def _gla_out_ffn_body(x_ref, o_ref, r_ref, ng_ref, wo_ref, g_ref, wgu_ref, wd_ref, y_ref):
    hv = ng_ref.shape[1]
    o = o_ref[...]
    r = r_ref[...].astype(F32)
    ng = ng_ref[...]
    parts = []
    for h in range(o.shape[1] // hv):
        sl = slice(h * hv, (h + 1) * hv)
        rh = r[:, sl]
        parts.append((_rmsnorm(o[:, sl], ng) * (rh * jax.nn.sigmoid(rh))).astype(BF16))
    gated = jnp.concatenate(parts, axis=-1)
    x = x_ref[...] + jnp.dot(gated, wo_ref[...], preferred_element_type=F32)
    y_ref[...] = _swiglu_residual(x, g_ref, wgu_ref, wd_ref)


def _gla_out_ffn(x, o, r, ng, w_out, g, w_gate_up, w_down):
    t, d = x.shape
    vd = o.shape[1]
    return pl.pallas_call(
        _gla_out_ffn_body,
        out_shape=jax.ShapeDtypeStruct((t, d), F32),
        grid=(t // ROW_TILE,),
        in_specs=[_rows(d), _rows(vd), _rows(vd), _resident(ng.shape), _resident(w_out.shape),
                  _resident((1, d)), _resident(w_gate_up.shape), _resident(w_down.shape)],
        out_specs=_rows(d),
        compiler_params=_params(1),
        name="gla_out_ffn",
    )(x, o, r, ng, w_out, g, w_gate_up, w_down)


def _sb_inproj_body(x_ref, g_ref, w_ref, gq_ref, gk_ref, q_ref, k_ref, v_ref):
    d = x_ref.shape[1]
    h = _rmsnorm(x_ref[...], g_ref[...]).astype(BF16)
    qkv = jnp.dot(h, w_ref[...], preferred_element_type=F32)
    first = lax.broadcasted_iota(jnp.int32, (1, LANES), 1) < SB_HEAD_DIM

    def head_norm(t, gain):
        sq = t * t
        ss_a = jnp.sum(jnp.where(first, sq, 0.0), axis=-1, keepdims=True)
        ss_b = jnp.sum(jnp.where(first, 0.0, sq), axis=-1, keepdims=True)
        inv = jnp.where(first,
                        lax.rsqrt(ss_a / SB_HEAD_DIM + NORM_EPS),
                        lax.rsqrt(ss_b / SB_HEAD_DIM + NORM_EPS))
        return (t * inv * gain).astype(BF16)

    gq = gq_ref[...] * (SB_HEAD_DIM ** -0.5 * LOG2E)
    gk = gk_ref[...]
    for c in range(d // LANES):
        cols = slice(c * LANES, (c + 1) * LANES)
        q_ref[:, cols] = head_norm(qkv[:, cols], gq)
        k_ref[:, cols] = head_norm(qkv[:, d + c * LANES:d + (c + 1) * LANES], gk)
    v_ref[...] = qkv[:, 2 * d:].astype(BF16)


def _sb_inproj(x, g, w, gq, gk):
    t, d = x.shape
    out = jax.ShapeDtypeStruct((t, d), BF16)
    return pl.pallas_call(
        _sb_inproj_body,
        out_shape=(out, out, out),
        grid=(t // INPROJ_ROW_TILE,),
        in_specs=[_rows(d, INPROJ_ROW_TILE), _resident((1, d)), _resident((d, 3 * d)),
                  _resident((1, LANES)), _resident((1, LANES))],
        out_specs=(_rows(d, INPROJ_ROW_TILE),) * 3,
        compiler_params=_params(1),
        name="sb_inproj",
    )(x, g, w, gq, gk)


def _softplus_bits(z):
    return jnp.maximum(z, 0.0) + jnp.log(1.0 + jnp.exp2(-jnp.abs(z))) * LOG2E


def _sb_attn_body(q_ref, k_ref, v_ref, o_ref, ka_ref, kb_ref, va_ref, vb_ref, acc_ref, rem_ref):
    seq = q_ref.shape[1]
    blk = SB_BLOCK
    nb = seq // blk
    first = lax.broadcasted_iota(jnp.int32, (1, LANES), 1) < SB_HEAD_DIM
    kk = k_ref[0]
    vv = v_ref[0]
    zero = jnp.zeros_like(kk)
    ka_ref[...] = jnp.where(first, kk, zero)
    kb_ref[...] = jnp.where(first, zero, kk)
    va_ref[...] = jnp.where(first, vv, zero)
    vb_ref[...] = jnp.where(first, zero, vv)

    row = lax.broadcasted_iota(jnp.int32, (blk, LANES), 0)
    col = lax.broadcasted_iota(jnp.int32, (blk, LANES), 1)
    strict = (col & (blk - 1)) < row
    src = lax.broadcasted_iota(jnp.int32, (LANES, 2 * LANES), 0)
    dst = lax.broadcasted_iota(jnp.int32, (LANES, 2 * LANES), 1)
    same_head = (src & blk) == (dst & blk)
    later_and_total = (same_head & ((dst >= LANES) | ((src & (blk - 1)) > (dst & (blk - 1))))
                       ).astype(BF16)

    def keys_values(keys):
        return (jnp.concatenate([ka_ref[keys, :], kb_ref[keys, :]], axis=0),
                jnp.concatenate([va_ref[keys, :], vb_ref[keys, :]], axis=0))

    def mask_first_block(x):
        masked = jnp.where(strict, x[:blk], 0.0)
        return masked if x.shape[0] == blk else jnp.concatenate([masked, x[blk:]], axis=0)

    def scores_stage(q_rows, k2, diagonal):
        z = lax.dot_general(q_rows, k2, _NT, preferred_element_type=F32)
        sp = _softplus_bits(z)
        spm = mask_first_block(sp) if diagonal else sp
        return z - sp, spm.astype(BF16)

    def sums_stage(spm):
        return jnp.dot(spm, later_and_total, preferred_element_type=F32)

    def weights_stage(log_beta, sums, rem, v2, diagonal):
        w = jnp.exp2(log_beta - sums[:, :LANES] - rem)
        if diagonal:
            w = mask_first_block(w)
        return jnp.dot(w.astype(BF16), v2, preferred_element_type=F32)

    def block_step(q_rows, k2, v2, rem):
        log_beta, spm = scores_stage(q_rows, k2, False)
        sums = sums_stage(spm)
        return weights_stage(log_beta, sums, rem, v2, False), sums[:, LANES:]

    span_lo = SB_PASS1_ROWS[1:] + (0,)
    carried = {}
    least_rem = {}
    key_blocks = list(range(nb - 1, -1, -1))
    keys_of = lambda kb: slice(kb * blk, (kb + 1) * blk)

    def rows_of(kb):
        offsets = [j for j in range(len(SB_PASS1_ROWS)) if kb + j < nb]
        return slice(kb * blk, (kb + offsets[-1]) * blk + SB_PASS1_ROWS[offsets[-1]]), offsets

    for g0 in range(0, nb, SB_GROUP):
        group = key_blocks[g0:g0 + SB_GROUP]
        operands = [keys_values(keys_of(kb)) for kb in group]
        staged = [scores_stage(q_ref[0,rows_of(kb)[0], :], k2, True)
                  for kb, (k2, _) in zip(group, operands)]
        sums = [sums_stage(spm) for _, spm in staged]
        for kb, (_, v2), (log_beta, _), unit_sums in zip(group, operands, staged, sums):
            rows, offsets = rows_of(kb)
            rem = jnp.concatenate([jnp.zeros((blk, LANES), F32)] + [carried[j] for j in offsets[1:]],
                                  axis=0)
            out = weights_stage(log_beta, unit_sums, rem, v2, True)
            acc_ref[keys_of(kb), :] = out[:blk]
            if len(offsets) > 1:
                acc_ref[rows.start + blk:rows.stop, :] += out[blk:]
            after = rem + unit_sums[:, LANES:]
            for j in offsets:
                rem_j = after[j * blk:j * blk + SB_PASS1_ROWS[j]]
                if j + 1 < len(SB_PASS1_ROWS):
                    carried[j + 1] = rem_j[:SB_PASS1_ROWS[j + 1]]
                if kb > 0 and span_lo[j] < SB_PASS1_ROWS[j]:
                    done = rem_j[span_lo[j]:]
                    rem_ref[(kb + j) * blk + span_lo[j]:(kb + j) * blk + SB_PASS1_ROWS[j], :] = done
                    least_rem[j] = jnp.minimum(least_rem[j], done) if j in least_rem else done

    def need_more(rem):
        return (jnp.min(rem) < SB_ZERO_WEIGHT).astype(jnp.int32)

    def continue_rows(j):
        n_rows = SB_PASS1_ROWS[j] - span_lo[j]

        def query_block(i, _):
            q_rows = pl.ds(pl.multiple_of(i * blk + span_lo[j], BF16_SUBLANES), n_rows)

            def body(c):
                kb, _, rem = c
                k2, v2 = keys_values(pl.ds(pl.multiple_of(kb * blk, blk), blk))
                out, total = block_step(q_ref[0, q_rows, :], k2, v2, rem)
                acc_ref[q_rows, :] += out
                rem = rem + total
                return kb - 1, need_more(rem), rem

            rem = rem_ref[q_rows, :]
            lax.while_loop(lambda c: (c[0] >= 0) & (c[1] > 0), body,
                           (i - j - 1, need_more(rem), rem))
            return 0

        lax.fori_loop(j + 1, nb, query_block, 0)

    for j, least in least_rem.items():
        @pl.when(jnp.min(least) < SB_ZERO_WEIGHT)
        def _():
            continue_rows(j)

    o_ref[0] = acc_ref[...].astype(BF16)


def _sb_attn(q, k, v):
    b, s, d = q.shape
    pairs = d // LANES
    blockspec = pl.BlockSpec((1, s, LANES), lambda i, j: (i, 0, j))
    return pl.pallas_call(
        _sb_attn_body,
        out_shape=jax.ShapeDtypeStruct((b, s, d), BF16),
        grid=(b, pairs),
        in_specs=[blockspec, blockspec, blockspec],
        out_specs=blockspec,
        scratch_shapes=[pltpu.VMEM((s, LANES), BF16)] * 4 + [pltpu.VMEM((s, LANES), F32)] * 2,
        compiler_params=_params(2),
        name="sb_attn",
    )(q, k, v)


def _sb_mixer(x, batch, attn_g, w_in, gq, gk):
    t, d = x.shape
    reps = LANES // SB_HEAD_DIM
    q, k, v = _sb_inproj(x, attn_g, w_in, jnp.tile(gq, reps)[None, :], jnp.tile(gk, reps)[None, :])
    shape3 = (batch, t // batch, d)
    o = _sb_attn(q.reshape(shape3), k.reshape(shape3), v.reshape(shape3))
    return o.reshape(t, d)


def _gla_inproj_body(x_ref, g_ref, w_ref, wg1_ref, wg2_ref, bg_ref,
                     q_ref, k_ref, v_ref, r_ref, la_ref, *, kd, vd, q_scale):
    h = _rmsnorm(x_ref[...], g_ref[...]).astype(BF16)
    proj = jnp.dot(h, w_ref[...], preferred_element_type=F32)
    q_ref[...] = (proj[:, :kd] * q_scale).astype(BF16)
    k_ref[...] = proj[:, kd:2 * kd].astype(BF16)
    v_ref[...] = proj[:, 2 * kd:2 * kd + vd].astype(BF16)
    r_ref[...] = proj[:, 2 * kd + vd:].astype(BF16)
    low_rank = jnp.dot(h, wg1_ref[...], preferred_element_type=F32).astype(BF16)
    gate = jnp.dot(low_rank, wg2_ref[...], preferred_element_type=F32) + bg_ref[...]
    softplus_neg = jnp.maximum(-gate, 0.0) + jnp.log(1.0 + jnp.exp(-jnp.abs(gate)))
    la_ref[...] = -softplus_neg / GLA_TAU


def _gla_inproj(x, g, w, wg1, wg2, bg, kd, vd):
    t, d = x.shape
    body = functools.partial(_gla_inproj_body, kd=kd, vd=vd,
                             q_scale=(kd // GLA_HEADS) ** -0.5)
    return pl.pallas_call(
        body,
        out_shape=(jax.ShapeDtypeStruct((t, kd), BF16), jax.ShapeDtypeStruct((t, kd), BF16),
                   jax.ShapeDtypeStruct((t, vd), BF16), jax.ShapeDtypeStruct((t, vd), BF16),
                   jax.ShapeDtypeStruct((t, kd), F32)),
        grid=(t // INPROJ_ROW_TILE,),
        in_specs=[_rows(d, INPROJ_ROW_TILE), _resident((1, d)), _resident(w.shape),
                  _resident(wg1.shape), _resident(wg2.shape), _resident((1, kd))],
        out_specs=tuple(_rows(width, INPROJ_ROW_TILE) for width in (kd, kd, vd, vd, kd)),
        compiler_params=_params(1),
        name="gla_inproj",
    )(x, g, w, wg1, wg2, bg)


def _gla_scan_body(q_ref, k_ref, la_ref, v_ref, o_ref, rowbuf_ref):
    hk = q_ref.shape[2] // GLA_HEADS_PER_STEP
    hv = v_ref.shape[2] // GLA_HEADS_PER_STEP
    for h in range(GLA_HEADS_PER_STEP):
        _gla_scan_head(q_ref, k_ref, la_ref, v_ref, o_ref, rowbuf_ref,
                       slice(h * hk, (h + 1) * hk), slice(h * hv, (h + 1) * hv))


def _gla_scan_head(q_ref, k_ref, la_ref, v_ref, o_ref, rowbuf_ref, key_lanes, value_lanes):
    seq = q_ref.shape[1]
    c = GLA_CHUNK
    row = lax.broadcasted_iota(jnp.int32, (c, c), 0)
    col = lax.broadcasted_iota(jnp.int32, (c, c), 1)
    causal = col <= row
    prefix = causal.astype(BF16)
    chunks = [slice(n * c, (n + 1) * c) for n in range(seq // c)]
    hk = key_lanes.stop - key_lanes.start

    def chunk_cum(la):
        la_hi, la_lo = _split_bf16(la)
        both = jnp.dot(prefix, jnp.concatenate([la_hi, la_lo], axis=1), preferred_element_type=F32)
        return both[:, :hk] + both[:, hk:]

    def factored_scores(qd, kf, cum):
        kd = (kf * jnp.exp(jnp.minimum(-cum, GLA_SAFE_DECAY))).astype(BF16)
        scores = lax.dot_general(qd, kd, _NT, preferred_element_type=F32)
        return jnp.where(causal, scores, 0.0).astype(BF16)

    cums = [chunk_cum(la_ref[0, rows, key_lanes]) for rows in chunks]
    qds, k_ends, decays, intra = [], [], [], []
    lowest = cums[0][c - 1:c, :]
    for rows, cum in zip(chunks, cums):
        qf = q_ref[0, rows, key_lanes].astype(F32)
        kf = k_ref[0, rows, key_lanes].astype(F32)
        last = cum[c - 1:c, :]
        lowest = jnp.minimum(lowest, last)
        qd = (qf * jnp.exp(cum)).astype(BF16)
        qds.append(qd)
        k_ends.append((kf * jnp.exp(last - cum)).astype(BF16))
        decays.append(jnp.exp(last))
        intra.append(jnp.dot(factored_scores(qd, kf, cum), v_ref[0, rows, value_lanes],
                             preferred_element_type=F32))
    updates = [lax.dot_general(v_ref[0, rows, value_lanes], k_end, _TN, preferred_element_type=F32)
               for rows, k_end in zip(chunks, k_ends)]
    state = None
    for n, rows in enumerate(chunks):
        out = intra[n]
        if state is not None:
            out = out + lax.dot_general(qds[n], state.astype(BF16), _NT, preferred_element_type=F32)
        o_ref[0, rows, value_lanes] = out
        state = updates[n] if state is None else state * decays[n] + updates[n]

    def redo_chunk(n, _):
        rows = pl.ds(pl.multiple_of(n * c, c), c)
        cum = chunk_cum(la_ref[0, rows, key_lanes])

        @pl.when(jnp.min(cum[c - 1:c, :]) < -GLA_SAFE_DECAY)
        def _():
            qf = q_ref[0, rows, key_lanes].astype(F32)
            kf = k_ref[0, rows, key_lanes].astype(F32)
            vv = v_ref[0, rows, value_lanes]
            rowbuf_ref[0] = kf
            rowbuf_ref[1] = cum

            def key_column(s, scores):
                k_s = rowbuf_ref[0, pl.ds(s, 1), :]
                cum_s = rowbuf_ref[1, pl.ds(s, 1), :]
                terms = qf * jnp.exp(jnp.minimum(cum - cum_s, 0.0)) * k_s
                column = jnp.sum(terms, axis=-1, keepdims=True)
                return jnp.where((col == s) & (row >= s), column, scores)

            exact = lax.fori_loop(0, c, key_column, jnp.zeros((c, c), F32)).astype(BF16)
            factored = factored_scores((qf * jnp.exp(cum)).astype(BF16), kf, cum)
            o_ref[0, rows, value_lanes] += (jnp.dot(exact, vv, preferred_element_type=F32)
                                  - jnp.dot(factored, vv, preferred_element_type=F32))
        return 0

    @pl.when(jnp.min(lowest) < -GLA_SAFE_DECAY)
    def _():
        lax.fori_loop(0, len(chunks), redo_chunk, 0)


def _gla_scan(q, k, la, v):
    b, s, kd = q.shape
    vd = v.shape[2]
    steps = GLA_HEADS // GLA_HEADS_PER_STEP
    key_spec = pl.BlockSpec((1, s, kd // steps), lambda i, j: (i, 0, j))
    val_spec = pl.BlockSpec((1, s, vd // steps), lambda i, j: (i, 0, j))
    return pl.pallas_call(
        _gla_scan_body,
        out_shape=jax.ShapeDtypeStruct((b, s, vd), F32),
        grid=(b, steps),
        in_specs=[key_spec, key_spec, key_spec, val_spec],
        out_specs=val_spec,
        scratch_shapes=[pltpu.VMEM((2, GLA_CHUNK, kd // GLA_HEADS), F32)],
        compiler_params=_params(2),
        name="gla_scan",
    )(q, k, la, v)


def _gla_mixer(x, batch, attn_g, w_in, w_gate2, b_gate, vd):
    t, d = x.shape
    kd = w_gate2.shape[1]
    main = 2 * kd + 2 * vd
    wg1 = jnp.pad(w_in[:, main:], ((0, 0), (0, LANES - GLA_GATE_RANK)))
    wg2 = jnp.pad(w_gate2, ((0, LANES - GLA_GATE_RANK), (0, 0)))
    q, k, v, r, la = _gla_inproj(x, attn_g, w_in[:, :main], wg1, wg2, b_gate[None, :], kd, vd)
    s = t // batch
    o = _gla_scan(q.reshape(batch, s, kd), k.reshape(batch, s, kd), la.reshape(batch, s, kd),
                  v.reshape(batch, s, vd))
    return o.reshape(t, vd), r


def kernel(x, attn_norm_g, ffn_norm_g, gla_w_in, gla_w_gate2, gla_b_gate, gla_norm_g, gla_w_out,
           sb_w_in, sb_q_norm_g, sb_k_norm_g, sb_w_out, ffn_w_gate_up, ffn_w_down):
    batch, seq, d = x.shape
    depth = attn_norm_g.shape[0]
    assert (batch * seq) % ROW_TILE == 0 and (batch * seq) % INPROJ_ROW_TILE == 0
    assert seq % GLA_CHUNK == 0 and seq % SB_BLOCK == 0
    bf = lambda w: w.astype(BF16)
    xt = x.reshape(batch * seq, d)
    for i in range(depth):
        j = i // 2
        attn_g = attn_norm_g[i][None, :]
        ffn = (ffn_norm_g[i][None, :], bf(ffn_w_gate_up[i]), bf(ffn_w_down[i]))
        if i % 2 == 0:
            o, r = _gla_mixer(xt, batch, attn_g, bf(gla_w_in[j]), bf(gla_w_gate2[j]),
                              gla_b_gate[j], gla_w_out.shape[1])
            xt = _gla_out_ffn(xt, o, r, gla_norm_g[j][None, :], bf(gla_w_out[j]), *ffn)
        else:
            o = _sb_mixer(xt, batch, attn_g, bf(sb_w_in[j]), sb_q_norm_g[j], sb_k_norm_g[j])
            xt = _sb_out_ffn(xt, o, bf(sb_w_out[j]), *ffn)
    return xt.reshape(batch, seq, d)
```

```python
import functools

import jax
import jax.numpy as jnp
from jax import lax
from jax.experimental import pallas as pl
from jax.experimental.pallas import tpu as pltpu

F32 = jnp.float32
BF16 = jnp.bfloat16

NORM_EPS = 1e-6
GLA_HEADS = 4
GLA_TAU = 16.0
GLA_GATE_RANK = 16
SB_HEAD_DIM = 64

LANES = 128
ROW_TILE = 512
INPROJ_ROW_TILE = 1024
GLA_CHUNK = 128
GLA_HEADS_PER_STEP = 2
GLA_SAFE_DECAY = 60.0
SB_BLOCK = LANES // 2
SB_PASS1_ROWS = (64, 64, 64, 32)
SB_GROUP = 4
BF16_SUBLANES = 16
SB_ZERO_WEIGHT = 150.0
LOG2E = 1.4426950408889634
VMEM_LIMIT = 56 * 1024 * 1024

_NT = (((1,), (1,)), ((), ()))
_TN = (((0,), (0,)), ((), ()))


def _rmsnorm(xf, g):
    return xf * lax.rsqrt(jnp.mean(xf * xf, axis=-1, keepdims=True) + NORM_EPS) * g


def _split_bf16(x):
    hi = x.astype(BF16)
    lo = (x - hi.astype(F32)).astype(BF16)
    return hi, lo


def _params(n_parallel):
    return pltpu.CompilerParams(
        dimension_semantics=("parallel",) * n_parallel,
        vmem_limit_bytes=VMEM_LIMIT,
    )


def _resident(shape):
    return pl.BlockSpec(shape, lambda *_: (0,) * len(shape), pipeline_mode=pl.Buffered(1))


def _layer_of(stacked, layer, width=None):
    shape = (stacked.shape[1], width or stacked.shape[2])
    return pl.BlockSpec((None,) + shape, lambda *_: (layer, 0, 0), pipeline_mode=pl.Buffered(1))


def _rows(width, tile=ROW_TILE):
    return pl.BlockSpec((tile, width), lambda i: (i, 0))


def _swiglu_residual(x, g_ref, wgu_ref, wd_ref):
    d_ff = wd_ref.shape[0]
    h = _rmsnorm(x, g_ref[...]).astype(BF16)
    gu = jnp.dot(h, wgu_ref[...], preferred_element_type=F32)
    gate = gu[:, :d_ff]
    up = gu[:, d_ff:]
    act = (gate * jax.nn.sigmoid(gate) * up).astype(BF16)
    return x + jnp.dot(act, wd_ref[...], preferred_element_type=F32)


def _sb_out_ffn_body(x_ref, o_ref, wo_ref, g_ref, wgu_ref, wd_ref, y_ref):
    x = x_ref[...] + jnp.dot(o_ref[...], wo_ref[...], preferred_element_type=F32)
    y_ref[...] = _swiglu_residual(x, g_ref, wgu_ref, wd_ref)


def _sb_out_ffn(x, o, w_out, mixer_layer, g, w_gate_up, w_down, layer):
    t, d = x.shape
    return pl.pallas_call(
        _sb_out_ffn_body,
        out_shape=jax.ShapeDtypeStruct((t, d), F32),
        grid=(t // ROW_TILE,),
        in_specs=[_rows(d), _rows(d), _layer_of(w_out, mixer_layer), _resident((1, d)),
                  _layer_of(w_gate_up, layer), _layer_of(w_down, layer)],
        out_specs=_rows(d),
        compiler_params=_params(1),
        name="sb_out_ffn",
    )(x, o, w_out, g, w_gate_up, w_down)


def _gla_out_ffn_body(x_ref, o_ref, r_ref, ng_ref, wo_ref, g_ref, wgu_ref, wd_ref, y_ref):
    hv = ng_ref.shape[1]
    o = o_ref[...].astype(F32)
    r = r_ref[...].astype(F32)
    ng = ng_ref[...]
    parts = []
    for h in range(o.shape[1] // hv):
        sl = slice(h * hv, (h + 1) * hv)
        rh = r[:, sl]
        parts.append((_rmsnorm(o[:, sl], ng) * (rh * jax.nn.sigmoid(rh))).astype(BF16))
    gated = jnp.concatenate(parts, axis=-1)
    x = x_ref[...] + jnp.dot(gated, wo_ref[...], preferred_element_type=F32)
    y_ref[...] = _swiglu_residual(x, g_ref, wgu_ref, wd_ref)


def _gla_out_ffn(x, o, r, ng, w_out, mixer_layer, g, w_gate_up, w_down, layer):
    t, d = x.shape
    vd = o.shape[1]
    return pl.pallas_call(
        _gla_out_ffn_body,
        out_shape=jax.ShapeDtypeStruct((t, d), F32),
        grid=(t // ROW_TILE,),
        in_specs=[_rows(d), _rows(vd), _rows(vd), _resident(ng.shape), _layer_of(w_out, mixer_layer),
                  _resident((1, d)), _layer_of(w_gate_up, layer), _layer_of(w_down, layer)],
        out_specs=_rows(d),
        compiler_params=_params(1),
        name="gla_out_ffn",
    )(x, o, r, ng, w_out, g, w_gate_up, w_down)


def _sb_inproj_body(x_ref, g_ref, w_ref, gq_ref, gk_ref, q_ref, k_ref, v_ref):
    d = x_ref.shape[1]
    h = _rmsnorm(x_ref[...], g_ref[...]).astype(BF16)
    qkv = jnp.dot(h, w_ref[...], preferred_element_type=F32)
    first = lax.broadcasted_iota(jnp.int32, (1, LANES), 1) < SB_HEAD_DIM

    def head_norm(t, gain):
        sq = t * t
        ss_a = jnp.sum(jnp.where(first, sq, 0.0), axis=-1, keepdims=True)
        ss_b = jnp.sum(jnp.where(first, 0.0, sq), axis=-1, keepdims=True)
        inv = jnp.where(first,
                        lax.rsqrt(ss_a / SB_HEAD_DIM + NORM_EPS),
                        lax.rsqrt(ss_b / SB_HEAD_DIM + NORM_EPS))
        return (t * inv * gain).astype(BF16)

    gq = gq_ref[...] * (SB_HEAD_DIM ** -0.5 * LOG2E)
    gk = gk_ref[...]
    for c in range(d // LANES):
        cols = slice(c * LANES, (c + 1) * LANES)
        q_ref[:, cols] = head_norm(qkv[:, cols], gq)
        k_ref[:, cols] = head_norm(qkv[:, d + c * LANES:d + (c + 1) * LANES], gk)
    v_ref[...] = qkv[:, 2 * d:].astype(BF16)


def _sb_inproj(x, g, w, layer, gq, gk):
    t, d = x.shape
    out = jax.ShapeDtypeStruct((t, d), BF16)
    return pl.pallas_call(
        _sb_inproj_body,
        out_shape=(out, out, out),
        grid=(t // INPROJ_ROW_TILE,),
        in_specs=[_rows(d, INPROJ_ROW_TILE), _resident((1, d)), _layer_of(w, layer),
                  _resident((1, LANES)), _resident((1, LANES))],
        out_specs=(_rows(d, INPROJ_ROW_TILE),) * 3,
        compiler_params=_params(1),
        name="sb_inproj",
    )(x, g, w, gq, gk)


def _softplus_bits(z):
    return jnp.maximum(z, 0.0) + jnp.log(1.0 + jnp.exp2(-jnp.abs(z))) * LOG2E


def _sb_attn_body(q_ref, k_ref, v_ref, o_ref, ka_ref, kb_ref, va_ref, vb_ref, acc_ref, rem_ref):
    seq = q_ref.shape[1]
    blk = SB_BLOCK
    nb = seq // blk
    first = lax.broadcasted_iota(jnp.int32, (1, LANES), 1) < SB_HEAD_DIM
    kk = k_ref[0]
    vv = v_ref[0]
    zero = jnp.zeros_like(kk)
    ka_ref[...] = jnp.where(first, kk, zero)
    kb_ref[...] = jnp.where(first, zero, kk)
    va_ref[...] = jnp.where(first, vv, zero)
    vb_ref[...] = jnp.where(first, zero, vv)

    row = lax.broadcasted_iota(jnp.int32, (blk, LANES), 0)
    col = lax.broadcasted_iota(jnp.int32, (blk, LANES), 1)
    strict = (col & (blk - 1)) < row
    src = lax.broadcasted_iota(jnp.int32, (LANES, 2 * LANES), 0)
    dst = lax.broadcasted_iota(jnp.int32, (LANES, 2 * LANES), 1)
    same_head = (src & blk) == (dst & blk)
    later_and_total = (same_head & ((dst >= LANES) | ((src & (blk - 1)) > (dst & (blk - 1))))
                       ).astype(BF16)

    def keys_values(keys):
        return (jnp.concatenate([ka_ref[keys, :], kb_ref[keys, :]], axis=0),
                jnp.concatenate([va_ref[keys, :], vb_ref[keys, :]], axis=0))

    def mask_first_block(x):
        masked = jnp.where(strict, x[:blk], 0.0)
        return masked if x.shape[0] == blk else jnp.concatenate([masked, x[blk:]], axis=0)

    def scores_stage(q_rows, k2, diagonal):
        z = lax.dot_general(q_rows, k2, _NT, preferred_element_type=F32)
        sp = _softplus_bits(z)
        spm = mask_first_block(sp) if diagonal else sp
        return z - sp, spm.astype(BF16)

    def sums_stage(spm):
        return jnp.dot(spm, later_and_total, preferred_element_type=F32)

    def weights_stage(log_beta, sums, rem, v2, diagonal):
        w = jnp.exp2(log_beta - sums[:, :LANES] - rem)
        if diagonal:
            w = mask_first_block(w)
        return jnp.dot(w.astype(BF16), v2, preferred_element_type=F32)

    def block_step(q_rows, k2, v2, rem):
        log_beta, spm = scores_stage(q_rows, k2, False)
        sums = sums_stage(spm)
        return weights_stage(log_beta, sums, rem, v2, False), sums[:, LANES:]

    span_lo = SB_PASS1_ROWS[1:] + (0,)
    carried = {}
    least_rem = {}
    key_blocks = list(range(nb - 1, -1, -1))
    keys_of = lambda kb: slice(kb * blk, (kb + 1) * blk)

    def rows_of(kb):
        offsets = [j for j in range(len(SB_PASS1_ROWS)) if kb + j < nb]
        return slice(kb * blk, (kb + offsets[-1]) * blk + SB_PASS1_ROWS[offsets[-1]]), offsets

    for g0 in range(0, nb, SB_GROUP):
        group = key_blocks[g0:g0 + SB_GROUP]
        operands = [keys_values(keys_of(kb)) for kb in group]
        staged = [scores_stage(q_ref[0,rows_of(kb)[0], :], k2, True)
                  for kb, (k2, _) in zip(group, operands)]
        sums = [sums_stage(spm) for _, spm in staged]
        for kb, (_, v2), (log_beta, _), unit_sums in zip(group, operands, staged, sums):
            rows, offsets = rows_of(kb)
            rem = jnp.concatenate([jnp.zeros((blk, LANES), F32)] + [carried[j] for j in offsets[1:]],
                                  axis=0)
            out = weights_stage(log_beta, unit_sums, rem, v2, True)
            acc_ref[keys_of(kb), :] = out[:blk]
            if len(offsets) > 1:
                acc_ref[rows.start + blk:rows.stop, :] += out[blk:]
            after = rem + unit_sums[:, LANES:]
            for j in offsets:
                rem_j = after[j * blk:j * blk + SB_PASS1_ROWS[j]]
                if j + 1 < len(SB_PASS1_ROWS):
                    carried[j + 1] = rem_j[:SB_PASS1_ROWS[j + 1]]
                if kb > 0 and span_lo[j] < SB_PASS1_ROWS[j]:
                    done = rem_j[span_lo[j]:]
                    rem_ref[(kb + j) * blk + span_lo[j]:(kb + j) * blk + SB_PASS1_ROWS[j], :] = done
                    least_rem[j] = jnp.minimum(least_rem[j], done) if j in least_rem else done

    def need_more(rem):
        return (jnp.min(rem) < SB_ZERO_WEIGHT).astype(jnp.int32)

    def continue_rows(j):
        n_rows = SB_PASS1_ROWS[j] - span_lo[j]

        def query_block(i, _):
            q_rows = pl.ds(pl.multiple_of(i * blk + span_lo[j], BF16_SUBLANES), n_rows)

            def body(c):
                kb, _, rem = c
                k2, v2 = keys_values(pl.ds(pl.multiple_of(kb * blk, blk), blk))
                out, total = block_step(q_ref[0, q_rows, :], k2, v2, rem)
                acc_ref[q_rows, :] += out
                rem = rem + total
                return kb - 1, need_more(rem), rem

            rem = rem_ref[q_rows, :]
            lax.while_loop(lambda c: (c[0] >= 0) & (c[1] > 0), body,
                           (i - j - 1, need_more(rem), rem))
            return 0

        lax.fori_loop(j + 1, nb, query_block, 0)

    for j, least in least_rem.items():
        @pl.when(jnp.min(least) < SB_ZERO_WEIGHT)
        def _():
            continue_rows(j)

    o_ref[0] = acc_ref[...].astype(BF16)


def _sb_attn(q, k, v):
    b, s, d = q.shape
    pairs = d // LANES
    blockspec = pl.BlockSpec((1, s, LANES), lambda i, j: (i, 0, j))
    return pl.pallas_call(
        _sb_attn_body,
        out_shape=jax.ShapeDtypeStruct((b, s, d), BF16),
        grid=(b, pairs),
        in_specs=[blockspec, blockspec, blockspec],
        out_specs=blockspec,
        scratch_shapes=[pltpu.VMEM((s, LANES), BF16)] * 4 + [pltpu.VMEM((s, LANES), F32)] * 2,
        compiler_params=_params(2),
        name="sb_attn",
    )(q, k, v)


def _sb_mixer(x, batch, attn_g, w_in, layer, gq, gk):
    t, d = x.shape
    reps = LANES // SB_HEAD_DIM
    q, k, v = _sb_inproj(x, attn_g, w_in, layer,
                         jnp.tile(gq, reps)[None, :], jnp.tile(gk, reps)[None, :])
    shape3 = (batch, t // batch, d)
    o = _sb_attn(q.reshape(shape3), k.reshape(shape3), v.reshape(shape3))
    return o.reshape(t, d)


def _gla_inproj_body(x_ref, g_ref, w_ref, wg1_ref, wg2_ref, bg_ref,
                     q_ref, k_ref, v_ref, r_ref, la_ref, *, kd, vd, q_scale):
    h = _rmsnorm(x_ref[...], g_ref[...]).astype(BF16)
    low_rank = jnp.dot(h, wg1_ref[...], preferred_element_type=F32).astype(BF16)
    gate = jnp.dot(low_rank, wg2_ref[...], preferred_element_type=F32) + bg_ref[...]
    softplus_neg = jnp.maximum(-gate, 0.0) + jnp.log(1.0 + jnp.exp(-jnp.abs(gate)))
    la_ref[...] = -softplus_neg / GLA_TAU
    proj = jnp.dot(h, w_ref[...], preferred_element_type=F32)
    q_ref[...] = (proj[:, :kd] * q_scale).astype(BF16)
    k_ref[...] = proj[:, kd:2 * kd].astype(BF16)
    v_ref[...] = proj[:, 2 * kd:2 * kd + vd].astype(BF16)
    r_ref[...] = proj[:, 2 * kd + vd:].astype(BF16)


def _gla_inproj(x, g, w, layer, wg1, wg2, bg, kd, vd):
    t, d = x.shape
    body = functools.partial(_gla_inproj_body, kd=kd, vd=vd,
                             q_scale=(kd // GLA_HEADS) ** -0.5)
    return pl.pallas_call(
        body,
        out_shape=(jax.ShapeDtypeStruct((t, kd), BF16), jax.ShapeDtypeStruct((t, kd), BF16),
                   jax.ShapeDtypeStruct((t, vd), BF16), jax.ShapeDtypeStruct((t, vd), BF16),
                   jax.ShapeDtypeStruct((t, kd), F32)),
        grid=(t // INPROJ_ROW_TILE,),
        in_specs=[_rows(d, INPROJ_ROW_TILE), _resident((1, d)), _layer_of(w, layer, 2 * kd + 2 * vd),
                  _resident(wg1.shape), _resident(wg2.shape), _resident((1, kd))],
        out_specs=tuple(_rows(width, INPROJ_ROW_TILE) for width in (kd, kd, vd, vd, kd)),
        compiler_params=_params(1),
        name="gla_inproj",
    )(x, g, w, wg1, wg2, bg)


def _gla_scan_body(q_ref, k_ref, la_ref, v_ref, o_ref, rowbuf_ref):
    hk = q_ref.shape[2] // GLA_HEADS_PER_STEP
    hv = v_ref.shape[2] // GLA_HEADS_PER_STEP
    for h in range(GLA_HEADS_PER_STEP):
        _gla_scan_head(q_ref, k_ref, la_ref, v_ref, o_ref, rowbuf_ref,
                       slice(h * hk, (h + 1) * hk), slice(h * hv, (h + 1) * hv))


def _gla_scan_head(q_ref, k_ref, la_ref, v_ref, o_ref, rowbuf_ref, key_lanes, value_lanes):
    seq = q_ref.shape[1]
    c = GLA_CHUNK
    row = lax.broadcasted_iota(jnp.int32, (c, c), 0)
    col = lax.broadcasted_iota(jnp.int32, (c, c), 1)
    causal = col <= row
    prefix = causal.astype(BF16)
    chunks = [slice(n * c, (n + 1) * c) for n in range(seq // c)]
    hk = key_lanes.stop - key_lanes.start

    def chunk_cum(la):
        la_hi, la_lo = _split_bf16(la)
        both = jnp.dot(prefix, jnp.concatenate([la_hi, la_lo], axis=1), preferred_element_type=F32)
        return both[:, :hk] + both[:, hk:]

    def factored_scores(qd, kf, cum):
        kd = (kf * jnp.exp(jnp.minimum(-cum, GLA_SAFE_DECAY))).astype(BF16)
        scores = lax.dot_general(qd, kd, _NT, preferred_element_type=F32)
        return jnp.where(causal, scores, 0.0).astype(BF16)

    cums = [chunk_cum(la_ref[0, rows, key_lanes]) for rows in chunks]
    qds, k_ends, decays, intra = [], [], [], []
    lowest = cums[0][c - 1:c, :]
    for rows, cum in zip(chunks, cums):
        qf = q_ref[0, rows, key_lanes].astype(F32)
        kf = k_ref[0, rows, key_lanes].astype(F32)
        last = cum[c - 1:c, :]
        lowest = jnp.minimum(lowest, last)
        qd = (qf * jnp.exp(cum)).astype(BF16)
        qds.append(qd)
        k_ends.append((kf * jnp.exp(last - cum)).astype(BF16))
        decays.append(jnp.exp(last))
        intra.append(jnp.dot(factored_scores(qd, kf, cum), v_ref[0, rows, value_lanes],
                             preferred_element_type=F32))
    updates = [lax.dot_general(v_ref[0, rows, value_lanes], k_end, _TN, preferred_element_type=F32)
               for rows, k_end in zip(chunks, k_ends)]
    state = None
    for n, rows in enumerate(chunks):
        out = intra[n]
        if state is not None:
            out = out + lax.dot_general(qds[n], state.astype(BF16), _NT, preferred_element_type=F32)
        o_ref[0, rows, value_lanes] = out.astype(o_ref.dtype)
        state = updates[n] if state is None else state * decays[n] + updates[n]

    def redo_chunk(n, _):
        rows = pl.ds(pl.multiple_of(n * c, c), c)
        cum = chunk_cum(la_ref[0, rows, key_lanes])

        @pl.when(jnp.min(cum[c - 1:c, :]) < -GLA_SAFE_DECAY)
        def _():
            qf = q_ref[0, rows, key_lanes].astype(F32)
            kf = k_ref[0, rows, key_lanes].astype(F32)
            vv = v_ref[0, rows, value_lanes]
            rowbuf_ref[0] = kf
            rowbuf_ref[1] = cum

            def key_column(s, scores):
                k_s = rowbuf_ref[0, pl.ds(s, 1), :]
                cum_s = rowbuf_ref[1, pl.ds(s, 1), :]
                terms = qf * jnp.exp(jnp.minimum(cum - cum_s, 0.0)) * k_s
                column = jnp.sum(terms, axis=-1, keepdims=True)
                return jnp.where((col == s) & (row >= s), column, scores)

            exact = lax.fori_loop(0, c, key_column, jnp.zeros((c, c), F32)).astype(BF16)
            factored = factored_scores((qf * jnp.exp(cum)).astype(BF16), kf, cum)
            correction = (jnp.dot(exact, vv, preferred_element_type=F32)
                          - jnp.dot(factored, vv, preferred_element_type=F32))
            o_ref[0, rows, value_lanes] = (o_ref[0, rows, value_lanes].astype(F32)
                                           + correction).astype(o_ref.dtype)
        return 0

    @pl.when(jnp.min(lowest) < -GLA_SAFE_DECAY)
    def _():
        lax.fori_loop(0, len(chunks), redo_chunk, 0)


def _gla_scan(q, k, la, v):
    b, s, kd = q.shape
    vd = v.shape[2]
    steps = GLA_HEADS // GLA_HEADS_PER_STEP
    key_spec = pl.BlockSpec((1, s, kd // steps), lambda i, j: (i, 0, j))
    val_spec = pl.BlockSpec((1, s, vd // steps), lambda i, j: (i, 0, j))
    return pl.pallas_call(
        _gla_scan_body,
        out_shape=jax.ShapeDtypeStruct((b, s, vd), BF16),
        grid=(b, steps),
        in_specs=[key_spec, key_spec, key_spec, val_spec],
        out_specs=val_spec,
        scratch_shapes=[pltpu.VMEM((2, GLA_CHUNK, kd // GLA_HEADS), F32)],
        compiler_params=_params(2),
        name="gla_scan",
    )(q, k, la, v)


def _gla_mixer(x, batch, attn_g, w_in, layer, w_gate2, b_gate, vd):
    t, d = x.shape
    kd = w_gate2.shape[1]
    main = 2 * kd + 2 * vd
    wg1 = jnp.pad(w_in[layer, :, main:], ((0, 0), (0, LANES - GLA_GATE_RANK)))
    wg2 = jnp.pad(w_gate2, ((0, LANES - GLA_GATE_RANK), (0, 0)))
    q, k, v, r, la = _gla_inproj(x, attn_g, w_in, layer, wg1, wg2, b_gate[None, :], kd, vd)
    s = t // batch
    o = _gla_scan(q.reshape(batch, s, kd), k.reshape(batch, s, kd), la.reshape(batch, s, kd),
                  v.reshape(batch, s, vd))
    return o.reshape(t, vd), r


def kernel(x, attn_norm_g, ffn_norm_g, gla_w_in, gla_w_gate2, gla_b_gate, gla_norm_g, gla_w_out,
           sb_w_in, sb_q_norm_g, sb_k_norm_g, sb_w_out, ffn_w_gate_up, ffn_w_down):
    batch, seq, d = x.shape
    depth = attn_norm_g.shape[0]
    assert (batch * seq) % ROW_TILE == 0 and (batch * seq) % INPROJ_ROW_TILE == 0
    assert seq % GLA_CHUNK == 0 and seq % SB_BLOCK == 0
    gla_w_in, gla_w_gate2, gla_w_out, sb_w_in, sb_w_out, ffn_w_gate_up, ffn_w_down = (
        w.astype(BF16) for w in (gla_w_in, gla_w_gate2, gla_w_out, sb_w_in, sb_w_out,
                                 ffn_w_gate_up, ffn_w_down))
    xt = x.reshape(batch * seq, d)
    for i in range(depth):
        j = i // 2
        attn_g = attn_norm_g[i][None, :]
        ffn = (ffn_norm_g[i][None, :], ffn_w_gate_up, ffn_w_down, i)
        if i % 2 == 0:
            o, r = _gla_mixer(xt, batch, attn_g, gla_w_in, j, gla_w_gate2[j], gla_b_gate[j],
                              gla_w_out.shape[1])
            xt = _gla_out_ffn(xt, o, r, gla_norm_g[j][None, :], gla_w_out, j, *ffn)
        else:
            o = _sb_mixer(xt, batch, attn_g, sb_w_in, j, sb_q_norm_g[j], sb_k_norm_g[j])
            xt = _sb_out_ffn(xt, o, sb_w_out, j, *ffn)
    return xt.reshape(batch, seq, d)
```

```python
import functools

import jax
import jax.numpy as jnp
from jax import lax
from jax.experimental import pallas as pl
from jax.experimental.pallas import tpu as pltpu

F32 = jnp.float32
BF16 = jnp.bfloat16

NORM_EPS = 1e-6
GLA_HEADS = 4
GLA_TAU = 16.0
GLA_GATE_RANK = 16
SB_HEAD_DIM = 64

LANES = 128
ROW_TILE = 512
INPROJ_ROW_TILE = 1024
GLA_CHUNK = 128
GLA_HEADS_PER_STEP = 2
GLA_SAFE_DECAY = 60.0
SB_BLOCK = LANES // 2
SB_PASS1_ROWS = (64, 64, 64, 32)
SB_GROUP = 4
SB_PAIRS_PER_STEP = 2
BF16_SUBLANES = 16
SB_ZERO_WEIGHT = 150.0
LOG2E = 1.4426950408889634
VMEM_LIMIT = 56 * 1024 * 1024

_NT = (((1,), (1,)), ((), ()))
_TN = (((0,), (0,)), ((), ()))


def _rmsnorm(xf, g):
    return xf * lax.rsqrt(jnp.mean(xf * xf, axis=-1, keepdims=True) + NORM_EPS) * g


def _split_bf16(x):
    hi = x.astype(BF16)
    lo = (x - hi.astype(F32)).astype(BF16)
    return hi, lo


def _params(n_parallel):
    return pltpu.CompilerParams(
        dimension_semantics=("parallel",) * n_parallel,
        vmem_limit_bytes=VMEM_LIMIT,
    )


def _resident(shape):
    return pl.BlockSpec(shape, lambda *_: (0,) * len(shape), pipeline_mode=pl.Buffered(1))


def _layer_of(stacked, layer, width=None):
    shape = (stacked.shape[1], width or stacked.shape[2])
    return pl.BlockSpec((None,) + shape, lambda *_: (layer, 0, 0), pipeline_mode=pl.Buffered(1))


def _rows(width, tile=ROW_TILE):
    return pl.BlockSpec((tile, width), lambda i: (i, 0))


def _swiglu_residual(x, g_ref, wgu_ref, wd_ref):
    d_ff = wd_ref.shape[0]
    h = _rmsnorm(x, g_ref[...]).astype(BF16)
    gu = jnp.dot(h, wgu_ref[...], preferred_element_type=F32)
    gate = gu[:, :d_ff]
    up = gu[:, d_ff:]
    act = (gate * jax.nn.sigmoid(gate) * up).astype(BF16)
    return x + jnp.dot(act, wd_ref[...], preferred_element_type=F32)


def _sb_out_ffn_body(x_ref, o_ref, wo_ref, g_ref, wgu_ref, wd_ref, y_ref):
    x = x_ref[...] + jnp.dot(o_ref[...], wo_ref[...], preferred_element_type=F32)
    y_ref[...] = _swiglu_residual(x, g_ref, wgu_ref, wd_ref)


def _sb_out_ffn(x, o, w_out, mixer_layer, g, w_gate_up, w_down, layer):
    t, d = x.shape
    return pl.pallas_call(
        _sb_out_ffn_body,
        out_shape=jax.ShapeDtypeStruct((t, d), F32),
        grid=(t // ROW_TILE,),
        in_specs=[_rows(d), _rows(d), _layer_of(w_out, mixer_layer), _resident((1, d)),
                  _layer_of(w_gate_up, layer), _layer_of(w_down, layer)],
        out_specs=_rows(d),
        compiler_params=_params(1),
        name="sb_out_ffn",
    )(x, o, w_out, g, w_gate_up, w_down)


def _gla_out_ffn_body(x_ref, o_ref, r_ref, ng_ref, wo_ref, g_ref, wgu_ref, wd_ref, y_ref):
    hv = ng_ref.shape[1]
    o = o_ref[...].astype(F32)
    r = r_ref[...].astype(F32)
    ng = ng_ref[...]
    parts = []
    for h in range(o.shape[1] // hv):
        sl = slice(h * hv, (h + 1) * hv)
        rh = r[:, sl]
        parts.append((_rmsnorm(o[:, sl], ng) * (rh * jax.nn.sigmoid(rh))).astype(BF16))
    gated = jnp.concatenate(parts, axis=-1)
    x = x_ref[...] + jnp.dot(gated, wo_ref[...], preferred_element_type=F32)
    y_ref[...] = _swiglu_residual(x, g_ref, wgu_ref, wd_ref)


def _gla_out_ffn(x, o, r, ng, w_out, mixer_layer, g, w_gate_up, w_down, layer):
    t, d = x.shape
    vd = o.shape[1]
    return pl.pallas_call(
        _gla_out_ffn_body,
        out_shape=jax.ShapeDtypeStruct((t, d), F32),
        grid=(t // ROW_TILE,),
        in_specs=[_rows(d), _rows(vd), _rows(vd), _resident(ng.shape), _layer_of(w_out, mixer_layer),
                  _resident((1, d)), _layer_of(w_gate_up, layer), _layer_of(w_down, layer)],
        out_specs=_rows(d),
        compiler_params=_params(1),
        name="gla_out_ffn",
    )(x, o, r, ng, w_out, g, w_gate_up, w_down)


def _sb_inproj_body(x_ref, g_ref, w_ref, gq_ref, gk_ref, q_ref, k_ref, v_ref):
    d = x_ref.shape[1]
    h = _rmsnorm(x_ref[...], g_ref[...]).astype(BF16)
    qkv = jnp.dot(h, w_ref[...], preferred_element_type=F32)
    first = lax.broadcasted_iota(jnp.int32, (1, LANES), 1) < SB_HEAD_DIM

    def head_norm(t, gain):
        sq = t * t
        ss_a = jnp.sum(jnp.where(first, sq, 0.0), axis=-1, keepdims=True)
        ss_b = jnp.sum(jnp.where(first, 0.0, sq), axis=-1, keepdims=True)
        inv = jnp.where(first,
                        lax.rsqrt(ss_a / SB_HEAD_DIM + NORM_EPS),
                        lax.rsqrt(ss_b / SB_HEAD_DIM + NORM_EPS))
        return (t * inv * gain).astype(BF16)

    gq = gq_ref[...] * (SB_HEAD_DIM ** -0.5 * LOG2E)
    gk = gk_ref[...]
    for c in range(d // LANES):
        cols = slice(c * LANES, (c + 1) * LANES)
        q_ref[:, cols] = head_norm(qkv[:, cols], gq)
        k_ref[:, cols] = head_norm(qkv[:, d + c * LANES:d + (c + 1) * LANES], gk)
    v_ref[...] = qkv[:, 2 * d:].astype(BF16)


def _sb_inproj(x, g, w, layer, gq, gk):
    t, d = x.shape
    out = jax.ShapeDtypeStruct((t, d), BF16)
    return pl.pallas_call(
        _sb_inproj_body,
        out_shape=(out, out, out),
        grid=(t // INPROJ_ROW_TILE,),
        in_specs=[_rows(d, INPROJ_ROW_TILE), _resident((1, d)), _layer_of(w, layer),
                  _resident((1, LANES)), _resident((1, LANES))],
        out_specs=(_rows(d, INPROJ_ROW_TILE),) * 3,
        compiler_params=_params(1),
        name="sb_inproj",
    )(x, g, w, gq, gk)


def _softplus_bits(z):
    return jnp.maximum(z, 0.0) + jnp.log(1.0 + jnp.exp2(-jnp.abs(z))) * LOG2E


def _sb_attn_body(q_ref, k_ref, v_ref, o_ref, ka_ref, kb_ref, va_ref, vb_ref, acc_ref, rem_ref):
    for p in range(SB_PAIRS_PER_STEP):
        _sb_attn_pair(q_ref, k_ref, v_ref, o_ref, ka_ref, kb_ref, va_ref, vb_ref, acc_ref, rem_ref,
                      slice(p * LANES, (p + 1) * LANES))


def _sb_attn_pair(q_ref, k_ref, v_ref, o_ref, ka_ref, kb_ref, va_ref, vb_ref, acc_ref, rem_ref,
                  pair_lanes):
    seq = q_ref.shape[1]
    blk = SB_BLOCK
    nb = seq // blk
    first = lax.broadcasted_iota(jnp.int32, (1, LANES), 1) < SB_HEAD_DIM
    kk = k_ref[0, :, pair_lanes]
    vv = v_ref[0, :, pair_lanes]
    zero = jnp.zeros_like(kk)
    ka_ref[...] = jnp.where(first, kk, zero)
    kb_ref[...] = jnp.where(first, zero, kk)
    va_ref[...] = jnp.where(first, vv, zero)
    vb_ref[...] = jnp.where(first, zero, vv)

    row = lax.broadcasted_iota(jnp.int32, (blk, LANES), 0)
    col = lax.broadcasted_iota(jnp.int32, (blk, LANES), 1)
    strict = (col & (blk - 1)) < row
    src = lax.broadcasted_iota(jnp.int32, (LANES, 2 * LANES), 0)
    dst = lax.broadcasted_iota(jnp.int32, (LANES, 2 * LANES), 1)
    same_head = (src & blk) == (dst & blk)
    later_and_total = (same_head & ((dst >= LANES) | ((src & (blk - 1)) > (dst & (blk - 1))))
                       ).astype(BF16)

    def keys_values(keys):
        return (jnp.concatenate([ka_ref[keys, :], kb_ref[keys, :]], axis=0),
                jnp.concatenate([va_ref[keys, :], vb_ref[keys, :]], axis=0))

    def mask_first_block(x):
        masked = jnp.where(strict, x[:blk], 0.0)
        return masked if x.shape[0] == blk else jnp.concatenate([masked, x[blk:]], axis=0)

    def scores_stage(q_rows, k2, diagonal):
        z = lax.dot_general(q_rows, k2, _NT, preferred_element_type=F32)
        sp = _softplus_bits(z)
        spm = mask_first_block(sp) if diagonal else sp
        return z - sp, spm.astype(BF16)

    def sums_stage(spm):
        return jnp.dot(spm, later_and_total, preferred_element_type=F32)

    def weights_stage(log_beta, sums, rem, v2, diagonal):
        w = jnp.exp2(log_beta - sums[:, :LANES] - rem)
        if diagonal:
            w = mask_first_block(w)
        return jnp.dot(w.astype(BF16), v2, preferred_element_type=F32)

    def block_step(q_rows, k2, v2, rem):
        log_beta, spm = scores_stage(q_rows, k2, False)
        sums = sums_stage(spm)
        return weights_stage(log_beta, sums, rem, v2, False), sums[:, LANES:]

    span_lo = SB_PASS1_ROWS[1:] + (0,)
    carried = {}
    least_rem = {}
    key_blocks = list(range(nb - 1, -1, -1))
    keys_of = lambda kb: slice(kb * blk, (kb + 1) * blk)

    def rows_of(kb):
        offsets = [j for j in range(len(SB_PASS1_ROWS)) if kb + j < nb]
        return slice(kb * blk, (kb + offsets[-1]) * blk + SB_PASS1_ROWS[offsets[-1]]), offsets

    for g0 in range(0, nb, SB_GROUP):
        group = key_blocks[g0:g0 + SB_GROUP]
        operands = [keys_values(keys_of(kb)) for kb in group]
        staged = [scores_stage(q_ref[0, rows_of(kb)[0], pair_lanes], k2, True)
                  for kb, (k2, _) in zip(group, operands)]
        sums = [sums_stage(spm) for _, spm in staged]
        for kb, (_, v2), (log_beta, _), unit_sums in zip(group, operands, staged, sums):
            rows, offsets = rows_of(kb)
            rem = jnp.concatenate([jnp.zeros((blk, LANES), F32)] + [carried[j] for j in offsets[1:]],
                                  axis=0)
            out = weights_stage(log_beta, unit_sums, rem, v2, True)
            acc_ref[keys_of(kb), :] = out[:blk]
            if len(offsets) > 1:
                acc_ref[rows.start + blk:rows.stop, :] += out[blk:]
            after = rem + unit_sums[:, LANES:]
            for j in offsets:
                rem_j = after[j * blk:j * blk + SB_PASS1_ROWS[j]]
                if j + 1 < len(SB_PASS1_ROWS):
                    carried[j + 1] = rem_j[:SB_PASS1_ROWS[j + 1]]
                if kb > 0 and span_lo[j] < SB_PASS1_ROWS[j]:
                    done = rem_j[span_lo[j]:]
                    rem_ref[(kb + j) * blk + span_lo[j]:(kb + j) * blk + SB_PASS1_ROWS[j], :] = done
                    least_rem[j] = jnp.minimum(least_rem[j], done) if j in least_rem else done

    def need_more(rem):
        return (jnp.min(rem) < SB_ZERO_WEIGHT).astype(jnp.int32)

    def continue_rows(j):
        n_rows = SB_PASS1_ROWS[j] - span_lo[j]

        def query_block(i, _):
            q_rows = pl.ds(pl.multiple_of(i * blk + span_lo[j], BF16_SUBLANES), n_rows)

            def body(c):
                kb, _, rem = c
                k2, v2 = keys_values(pl.ds(pl.multiple_of(kb * blk, blk), blk))
                out, total = block_step(q_ref[0, q_rows, pair_lanes], k2, v2, rem)
                acc_ref[q_rows, :] += out
                rem = rem + total
                return kb - 1, need_more(rem), rem

            rem = rem_ref[q_rows, :]
            lax.while_loop(lambda c: (c[0] >= 0) & (c[1] > 0), body,
                           (i - j - 1, need_more(rem), rem))
            return 0

        lax.fori_loop(j + 1, nb, query_block, 0)

    for j, least in least_rem.items():
        @pl.when(jnp.min(least) < SB_ZERO_WEIGHT)
        def _():
            continue_rows(j)

    o_ref[0, :, pair_lanes] = acc_ref[...].astype(BF16)


def _sb_attn(q, k, v):
    b, s, d = q.shape
    width = SB_PAIRS_PER_STEP * LANES
    pairs = d // width
    blockspec = pl.BlockSpec((1, s, width), lambda i, j: (i, 0, j))
    return pl.pallas_call(
        _sb_attn_body,
        out_shape=jax.ShapeDtypeStruct((b, s, d), BF16),
        grid=(b, pairs),
        in_specs=[blockspec, blockspec, blockspec],
        out_specs=blockspec,
        scratch_shapes=[pltpu.VMEM((s, LANES), BF16)] * 4 + [pltpu.VMEM((s, LANES), F32)] * 2,
        compiler_params=_params(2),
        name="sb_attn",
    )(q, k, v)


def _sb_mixer(x, batch, attn_g, w_in, layer, gq, gk):
    t, d = x.shape
    reps = LANES // SB_HEAD_DIM
    q, k, v = _sb_inproj(x, attn_g, w_in, layer,
                         jnp.tile(gq, reps)[None, :], jnp.tile(gk, reps)[None, :])
    shape3 = (batch, t // batch, d)
    o = _sb_attn(q.reshape(shape3), k.reshape(shape3), v.reshape(shape3))
    return o.reshape(t, d)


def _gla_inproj_body(x_ref, g_ref, w_ref, wg1_ref, wg2_ref, bg_ref,
                     q_ref, k_ref, v_ref, r_ref, la_ref, *, kd, vd, q_scale):
    h = _rmsnorm(x_ref[...], g_ref[...]).astype(BF16)
    low_rank = jnp.dot(h, wg1_ref[...], preferred_element_type=F32).astype(BF16)
    gate = jnp.dot(low_rank, wg2_ref[...], preferred_element_type=F32) + bg_ref[...]
    softplus_neg = jnp.maximum(-gate, 0.0) + jnp.log(1.0 + jnp.exp(-jnp.abs(gate)))
    la_ref[...] = -softplus_neg / GLA_TAU
    proj = jnp.dot(h, w_ref[...], preferred_element_type=F32)
    q_ref[...] = (proj[:, :kd] * q_scale).astype(BF16)
    k_ref[...] = proj[:, kd:2 * kd].astype(BF16)
    v_ref[...] = proj[:, 2 * kd:2 * kd + vd].astype(BF16)
    r_ref[...] = proj[:, 2 * kd + vd:].astype(BF16)


def _gla_inproj(x, g, w, layer, wg1, wg2, bg, kd, vd):
    t, d = x.shape
    body = functools.partial(_gla_inproj_body, kd=kd, vd=vd,
                             q_scale=(kd // GLA_HEADS) ** -0.5)
    return pl.pallas_call(
        body,
        out_shape=(jax.ShapeDtypeStruct((t, kd), BF16), jax.ShapeDtypeStruct((t, kd), BF16),
                   jax.ShapeDtypeStruct((t, vd), BF16), jax.ShapeDtypeStruct((t, vd), BF16),
                   jax.ShapeDtypeStruct((t, kd), F32)),
        grid=(t // INPROJ_ROW_TILE,),
        in_specs=[_rows(d, INPROJ_ROW_TILE), _resident((1, d)), _layer_of(w, layer, 2 * kd + 2 * vd),
                  _resident(wg1.shape), _resident(wg2.shape), _resident((1, kd))],
        out_specs=tuple(_rows(width, INPROJ_ROW_TILE) for width in (kd, kd, vd, vd, kd)),
        compiler_params=_params(1),
        name="gla_inproj",
    )(x, g, w, wg1, wg2, bg)


def _gla_scan_body(q_ref, k_ref, la_ref, v_ref, o_ref, rowbuf_ref):
    hk = q_ref.shape[2] // GLA_HEADS_PER_STEP
    hv = v_ref.shape[2] // GLA_HEADS_PER_STEP
    for h in range(GLA_HEADS_PER_STEP):
        _gla_scan_head(q_ref, k_ref, la_ref, v_ref, o_ref, rowbuf_ref,
                       slice(h * hk, (h + 1) * hk), slice(h * hv, (h + 1) * hv))


def _gla_scan_head(q_ref, k_ref, la_ref, v_ref, o_ref, rowbuf_ref, key_lanes, value_lanes):
    seq = q_ref.shape[1]
    c = GLA_CHUNK
    row = lax.broadcasted_iota(jnp.int32, (c, c), 0)
    col = lax.broadcasted_iota(jnp.int32, (c, c), 1)
    causal = col <= row
    prefix = causal.astype(BF16)
    chunks = [slice(n * c, (n + 1) * c) for n in range(seq // c)]
    hk = key_lanes.stop - key_lanes.start

    def chunk_cum(la):
        la_hi, la_lo = _split_bf16(la)
        both = jnp.dot(prefix, jnp.concatenate([la_hi, la_lo], axis=1), preferred_element_type=F32)
        return both[:, :hk] + both[:, hk:]

    def factored_scores(qd, kf, cum):
        kd = (kf * jnp.exp(jnp.minimum(-cum, GLA_SAFE_DECAY))).astype(BF16)
        scores = lax.dot_general(qd, kd, _NT, preferred_element_type=F32)
        return jnp.where(causal, scores, 0.0).astype(BF16)

    cums = [chunk_cum(la_ref[0, rows, key_lanes]) for rows in chunks]
    qds, k_ends, decays, intra = [], [], [], []
    lowest = cums[0][c - 1:c, :]
    for rows, cum in zip(chunks, cums):
        qf = q_ref[0, rows, key_lanes].astype(F32)
        kf = k_ref[0, rows, key_lanes].astype(F32)
        last = cum[c - 1:c, :]
        lowest = jnp.minimum(lowest, last)
        qd = (qf * jnp.exp(cum)).astype(BF16)
        qds.append(qd)
        k_ends.append((kf * jnp.exp(last - cum)).astype(BF16))
        decays.append(jnp.exp(last))
        intra.append(jnp.dot(factored_scores(qd, kf, cum), v_ref[0, rows, value_lanes],
                             preferred_element_type=F32))
    updates = [lax.dot_general(v_ref[0, rows, value_lanes], k_end, _TN, preferred_element_type=F32)
               for rows, k_end in zip(chunks, k_ends)]
    state = None
    for n, rows in enumerate(chunks):
        out = intra[n]
        if state is not None:
            out = out + lax.dot_general(qds[n], state.astype(BF16), _NT, preferred_element_type=F32)
        o_ref[0, rows, value_lanes] = out.astype(o_ref.dtype)
        state = updates[n] if state is None else state * decays[n] + updates[n]

    def redo_chunk(n, _):
        rows = pl.ds(pl.multiple_of(n * c, c), c)
        cum = chunk_cum(la_ref[0, rows, key_lanes])

        @pl.when(jnp.min(cum[c - 1:c, :]) < -GLA_SAFE_DECAY)
        def _():
            qf = q_ref[0, rows, key_lanes].astype(F32)
            kf = k_ref[0, rows, key_lanes].astype(F32)
            vv = v_ref[0, rows, value_lanes]
            rowbuf_ref[0] = kf
            rowbuf_ref[1] = cum

            def key_column(s, scores):
                k_s = rowbuf_ref[0, pl.ds(s, 1), :]
                cum_s = rowbuf_ref[1, pl.ds(s, 1), :]
                terms = qf * jnp.exp(jnp.minimum(cum - cum_s, 0.0)) * k_s
                column = jnp.sum(terms, axis=-1, keepdims=True)
                return jnp.where((col == s) & (row >= s), column, scores)

            exact = lax.fori_loop(0, c, key_column, jnp.zeros((c, c), F32)).astype(BF16)
            factored = factored_scores((qf * jnp.exp(cum)).astype(BF16), kf, cum)
            correction = (jnp.dot(exact, vv, preferred_element_type=F32)
                          - jnp.dot(factored, vv, preferred_element_type=F32))
            o_ref[0, rows, value_lanes] = (o_ref[0, rows, value_lanes].astype(F32)
                                           + correction).astype(o_ref.dtype)
        return 0

    @pl.when(jnp.min(lowest) < -GLA_SAFE_DECAY)
    def _():
        lax.fori_loop(0, len(chunks), redo_chunk, 0)


def _gla_scan(q, k, la, v):
    b, s, kd = q.shape
    vd = v.shape[2]
    steps = GLA_HEADS // GLA_HEADS_PER_STEP
    key_spec = pl.BlockSpec((1, s, kd // steps), lambda i, j: (i, 0, j))
    val_spec = pl.BlockSpec((1, s, vd // steps), lambda i, j: (i, 0, j))
    return pl.pallas_call(
        _gla_scan_body,
        out_shape=jax.ShapeDtypeStruct((b, s, vd), BF16),
        grid=(b, steps),
        in_specs=[key_spec, key_spec, key_spec, val_spec],
        out_specs=val_spec,
        scratch_shapes=[pltpu.VMEM((2, GLA_CHUNK, kd // GLA_HEADS), F32)],
        compiler_params=_params(2),
        name="gla_scan",
    )(q, k, la, v)


def _gla_mixer(x, batch, attn_g, w_in, layer, w_gate2, b_gate, vd):
    t, d = x.shape
    kd = w_gate2.shape[1]
    main = 2 * kd + 2 * vd
    wg1 = jnp.pad(w_in[layer, :, main:], ((0, 0), (0, LANES - GLA_GATE_RANK)))
    wg2 = jnp.pad(w_gate2, ((0, LANES - GLA_GATE_RANK), (0, 0)))
    q, k, v, r, la = _gla_inproj(x, attn_g, w_in, layer, wg1, wg2, b_gate[None, :], kd, vd)
    s = t // batch
    o = _gla_scan(q.reshape(batch, s, kd), k.reshape(batch, s, kd), la.reshape(batch, s, kd),
                  v.reshape(batch, s, vd))
    return o.reshape(t, vd), r


def kernel(x, attn_norm_g, ffn_norm_g, gla_w_in, gla_w_gate2, gla_b_gate, gla_norm_g, gla_w_out,
           sb_w_in, sb_q_norm_g, sb_k_norm_g, sb_w_out, ffn_w_gate_up, ffn_w_down):
    batch, seq, d = x.shape
    depth = attn_norm_g.shape[0]
    assert (batch * seq) % ROW_TILE == 0 and (batch * seq) % INPROJ_ROW_TILE == 0
    assert seq % GLA_CHUNK == 0 and seq % SB_BLOCK == 0
    gla_w_in, gla_w_gate2, gla_w_out, sb_w_in, sb_w_out, ffn_w_gate_up, ffn_w_down = (
        w.astype(BF16) for w in (gla_w_in, gla_w_gate2, gla_w_out, sb_w_in, sb_w_out,
                                 ffn_w_gate_up, ffn_w_down))
    xt = x.reshape(batch * seq, d)
    for i in range(depth):
        j = i // 2
        attn_g = attn_norm_g[i][None, :]
        ffn = (ffn_norm_g[i][None, :], ffn_w_gate_up, ffn_w_down, i)
        if i % 2 == 0:
            o, r = _gla_mixer(xt, batch, attn_g, gla_w_in, j, gla_w_gate2[j], gla_b_gate[j],
                              gla_w_out.shape[1])
            xt = _gla_out_ffn(xt, o, r, gla_norm_g[j][None, :], gla_w_out, j, *ffn)
        else:
            o = _sb_mixer(xt, batch, attn_g, sb_w_in, j, sb_q_norm_g[j], sb_k_norm_g[j])
            xt = _sb_out_ffn(xt, o, sb_w_out, j, *ffn)
    return xt.reshape(batch, seq, d)
```

```python
import functools

import jax
import jax.numpy as jnp
from jax import lax
from jax.experimental import pallas as pl
from jax.experimental.pallas import tpu as pltpu

F32 = jnp.float32
BF16 = jnp.bfloat16

NORM_EPS = 1e-6
GLA_HEADS = 4
GLA_TAU = 16.0
GLA_GATE_RANK = 16
SB_HEAD_DIM = 64

LANES = 128
ROW_TILE = 512
INPROJ_ROW_TILE = 1024
GLA_CHUNK = 128
GLA_HEADS_PER_STEP = 4
GLA_SAFE_DECAY = 60.0
SB_BLOCK = LANES // 2
SB_PASS1_ROWS = (64, 64, 64, 32)
SB_GROUP = 4
BF16_SUBLANES = 16
SB_ZERO_WEIGHT = 150.0
LOG2E = 1.4426950408889634
VMEM_LIMIT = 56 * 1024 * 1024

_NT = (((1,), (1,)), ((), ()))
_TN = (((0,), (0,)), ((), ()))


def _rmsnorm(xf, g):
    return xf * lax.rsqrt(jnp.mean(xf * xf, axis=-1, keepdims=True) + NORM_EPS) * g


def _split_bf16(x):
    hi = x.astype(BF16)
    lo = (x - hi.astype(F32)).astype(BF16)
    return hi, lo


def _params(n_parallel):
    return pltpu.CompilerParams(
        dimension_semantics=("parallel",) * n_parallel,
        vmem_limit_bytes=VMEM_LIMIT,
    )


def _resident(shape):
    return pl.BlockSpec(shape, lambda *_: (0,) * len(shape), pipeline_mode=pl.Buffered(1))


def _layer_of(stacked, layer, width=None):
    shape = (stacked.shape[1], width or stacked.shape[2])
    return pl.BlockSpec((None,) + shape, lambda *_: (layer, 0, 0), pipeline_mode=pl.Buffered(1))


def _rows(width, tile=ROW_TILE):
    return pl.BlockSpec((tile, width), lambda i: (i, 0))


def _swiglu_residual(x, g_ref, wgu_ref, wd_ref):
    d_ff = wd_ref.shape[0]
    h = _rmsnorm(x, g_ref[...]).astype(BF16)
    gu = jnp.dot(h, wgu_ref[...], preferred_element_type=F32)
    gate = gu[:, :d_ff]
    up = gu[:, d_ff:]
    act = (gate * jax.nn.sigmoid(gate) * up).astype(BF16)
    return x + jnp.dot(act, wd_ref[...], preferred_element_type=F32)


def _sb_out_ffn_body(x_ref, o_ref, wo_ref, g_ref, wgu_ref, wd_ref, y_ref):
    x = x_ref[...] + jnp.dot(o_ref[...], wo_ref[...], preferred_element_type=F32)
    y_ref[...] = _swiglu_residual(x, g_ref, wgu_ref, wd_ref)


def _sb_out_ffn(x, o, w_out, mixer_layer, g, w_gate_up, w_down, layer):
    t, d = x.shape
    return pl.pallas_call(
        _sb_out_ffn_body,
        out_shape=jax.ShapeDtypeStruct((t, d), F32),
        grid=(t // ROW_TILE,),
        in_specs=[_rows(d), _rows(d), _layer_of(w_out, mixer_layer), _resident((1, d)),
                  _layer_of(w_gate_up, layer), _layer_of(w_down, layer)],
        out_specs=_rows(d),
        compiler_params=_params(1),
        name="sb_out_ffn",
    )(x, o, w_out, g, w_gate_up, w_down)


def _gla_out_ffn_body(x_ref, o_ref, r_ref, ng_ref, wo_ref, g_ref, wgu_ref, wd_ref, y_ref):
    hv = ng_ref.shape[1]
    o = o_ref[...].astype(F32)
    r = r_ref[...].astype(F32)
    ng = ng_ref[...]
    parts = []
    for h in range(o.shape[1] // hv):
        sl = slice(h * hv, (h + 1) * hv)
        rh = r[:, sl]
        parts.append((_rmsnorm(o[:, sl], ng) * (rh * jax.nn.sigmoid(rh))).astype(BF16))
    gated = jnp.concatenate(parts, axis=-1)
    x = x_ref[...] + jnp.dot(gated, wo_ref[...], preferred_element_type=F32)
    y_ref[...] = _swiglu_residual(x, g_ref, wgu_ref, wd_ref)


def _gla_out_ffn(x, o, r, ng, w_out, mixer_layer, g, w_gate_up, w_down, layer):
    t, d = x.shape
    vd = o.shape[1]
    return pl.pallas_call(
        _gla_out_ffn_body,
        out_shape=jax.ShapeDtypeStruct((t, d), F32),
        grid=(t // ROW_TILE,),
        in_specs=[_rows(d), _rows(vd), _rows(vd), _resident(ng.shape), _layer_of(w_out, mixer_layer),
                  _resident((1, d)), _layer_of(w_gate_up, layer), _layer_of(w_down, layer)],
        out_specs=_rows(d),
        compiler_params=_params(1),
        name="gla_out_ffn",
    )(x, o, r, ng, w_out, g, w_gate_up, w_down)


def _sb_inproj_body(x_ref, g_ref, w_ref, gq_ref, gk_ref, q_ref, k_ref, v_ref):
    d = x_ref.shape[1]
    h = _rmsnorm(x_ref[...], g_ref[...]).astype(BF16)
    qkv = jnp.dot(h, w_ref[...], preferred_element_type=F32)
    first = lax.broadcasted_iota(jnp.int32, (1, LANES), 1) < SB_HEAD_DIM

    def head_norm(t, gain):
        sq = t * t
        ss_a = jnp.sum(jnp.where(first, sq, 0.0), axis=-1, keepdims=True)
        ss_b = jnp.sum(jnp.where(first, 0.0, sq), axis=-1, keepdims=True)
        inv = jnp.where(first,
                        lax.rsqrt(ss_a / SB_HEAD_DIM + NORM_EPS),
                        lax.rsqrt(ss_b / SB_HEAD_DIM + NORM_EPS))
        return (t * inv * gain).astype(BF16)

    gq = gq_ref[...] * (SB_HEAD_DIM ** -0.5 * LOG2E)
    gk = gk_ref[...]
    for c in range(d // LANES):
        cols = slice(c * LANES, (c + 1) * LANES)
        q_ref[:, cols] = head_norm(qkv[:, cols], gq)
        k_ref[:, cols] = head_norm(qkv[:, d + c * LANES:d + (c + 1) * LANES], gk)
    v_ref[...] = qkv[:, 2 * d:].astype(BF16)


def _sb_inproj(x, g, w, layer, gq, gk):
    t, d = x.shape
    out = jax.ShapeDtypeStruct((t, d), BF16)
    return pl.pallas_call(
        _sb_inproj_body,
        out_shape=(out, out, out),
        grid=(t // INPROJ_ROW_TILE,),
        in_specs=[_rows(d, INPROJ_ROW_TILE), _resident((1, d)), _layer_of(w, layer),
                  _resident((1, LANES)), _resident((1, LANES))],
        out_specs=(_rows(d, INPROJ_ROW_TILE),) * 3,
        compiler_params=_params(1),
        name="sb_inproj",
    )(x, g, w, gq, gk)


def _softplus_bits(z):
    return jnp.maximum(z, 0.0) + jnp.log(1.0 + jnp.exp2(-jnp.abs(z))) * LOG2E


def _sb_attn_body(q_ref, k_ref, v_ref, o_ref, ka_ref, kb_ref, va_ref, vb_ref, acc_ref, rem_ref):
    seq = q_ref.shape[1]
    blk = SB_BLOCK
    nb = seq // blk
    first = lax.broadcasted_iota(jnp.int32, (1, LANES), 1) < SB_HEAD_DIM
    kk = k_ref[0]
    vv = v_ref[0]
    zero = jnp.zeros_like(kk)
    ka_ref[...] = jnp.where(first, kk, zero)
    kb_ref[...] = jnp.where(first, zero, kk)
    va_ref[...] = jnp.where(first, vv, zero)
    vb_ref[...] = jnp.where(first, zero, vv)

    row = lax.broadcasted_iota(jnp.int32, (blk, LANES), 0)
    col = lax.broadcasted_iota(jnp.int32, (blk, LANES), 1)
    strict = (col & (blk - 1)) < row
    src = lax.broadcasted_iota(jnp.int32, (LANES, 2 * LANES), 0)
    dst = lax.broadcasted_iota(jnp.int32, (LANES, 2 * LANES), 1)
    same_head = (src & blk) == (dst & blk)
    later_and_total = (same_head & ((dst >= LANES) | ((src & (blk - 1)) > (dst & (blk - 1))))
                       ).astype(BF16)

    def keys_values(keys):
        return (jnp.concatenate([ka_ref[keys, :], kb_ref[keys, :]], axis=0),
                jnp.concatenate([va_ref[keys, :], vb_ref[keys, :]], axis=0))

    def mask_first_block(x):
        masked = jnp.where(strict, x[:blk], 0.0)
        return masked if x.shape[0] == blk else jnp.concatenate([masked, x[blk:]], axis=0)

    def scores_stage(q_rows, k2, diagonal):
        z = lax.dot_general(q_rows, k2, _NT, preferred_element_type=F32)
        sp = _softplus_bits(z)
        spm = mask_first_block(sp) if diagonal else sp
        return z - sp, spm.astype(BF16)

    def sums_stage(spm):
        return jnp.dot(spm, later_and_total, preferred_element_type=F32)

    def weights_stage(log_beta, sums, rem, v2, diagonal):
        w = jnp.exp2(log_beta - sums[:, :LANES] - rem)
        if diagonal:
            w = mask_first_block(w)
        return jnp.dot(w.astype(BF16), v2, preferred_element_type=F32)

    def block_step(q_rows, k2, v2, rem):
        log_beta, spm = scores_stage(q_rows, k2, False)
        sums = sums_stage(spm)
        return weights_stage(log_beta, sums, rem, v2, False), sums[:, LANES:]

    span_lo = SB_PASS1_ROWS[1:] + (0,)
    carried = {}
    least_rem = {}
    key_blocks = list(range(nb - 1, -1, -1))
    keys_of = lambda kb: slice(kb * blk, (kb + 1) * blk)

    def rows_of(kb):
        offsets = [j for j in range(len(SB_PASS1_ROWS)) if kb + j < nb]
        return slice(kb * blk, (kb + offsets[-1]) * blk + SB_PASS1_ROWS[offsets[-1]]), offsets

    def group_scores(group):
        operands = [keys_values(keys_of(kb)) for kb in group]
        staged = [scores_stage(q_ref[0, rows_of(kb)[0], :], k2, True)
                  for kb, (k2, _) in zip(group, operands)]
        return operands, staged

    def group_weights(group, operands, staged, sums):
        for kb, (_, v2), (log_beta, _), unit_sums in zip(group, operands, staged, sums):
            rows, offsets = rows_of(kb)
            rem = jnp.concatenate([jnp.zeros((blk, LANES), F32)] + [carried[j] for j in offsets[1:]],
                                  axis=0)
            out = weights_stage(log_beta, unit_sums, rem, v2, True)
            acc_ref[keys_of(kb), :] = out[:blk]
            if len(offsets) > 1:
                acc_ref[rows.start + blk:rows.stop, :] += out[blk:]
            after = rem + unit_sums[:, LANES:]
            for j in offsets:
                rem_j = after[j * blk:j * blk + SB_PASS1_ROWS[j]]
                if j + 1 < len(SB_PASS1_ROWS):
                    carried[j + 1] = rem_j[:SB_PASS1_ROWS[j + 1]]
                if kb > 0 and span_lo[j] < SB_PASS1_ROWS[j]:
                    done = rem_j[span_lo[j]:]
                    rem_ref[(kb + j) * blk + span_lo[j]:(kb + j) * blk + SB_PASS1_ROWS[j], :] = done
                    least_rem[j] = jnp.minimum(least_rem[j], done) if j in least_rem else done

    groups = [key_blocks[g0:g0 + SB_GROUP] for g0 in range(0, nb, SB_GROUP)]
    operands, staged = group_scores(groups[0])
    sums = [sums_stage(spm) for _, spm in staged]
    for g, group in enumerate(groups):
        upcoming = group_scores(groups[g + 1]) if g + 1 < len(groups) else None
        group_weights(group, operands, staged, sums)
        if upcoming is not None:
            operands, staged = upcoming
            sums = [sums_stage(spm) for _, spm in staged]

    def need_more(rem):
        return (jnp.min(rem) < SB_ZERO_WEIGHT).astype(jnp.int32)

    def continue_rows(j):
        n_rows = SB_PASS1_ROWS[j] - span_lo[j]

        def query_block(i, _):
            q_rows = pl.ds(pl.multiple_of(i * blk + span_lo[j], BF16_SUBLANES), n_rows)

            def body(c):
                kb, _, rem = c
                k2, v2 = keys_values(pl.ds(pl.multiple_of(kb * blk, blk), blk))
                out, total = block_step(q_ref[0, q_rows, :], k2, v2, rem)
                acc_ref[q_rows, :] += out
                rem = rem + total
                return kb - 1, need_more(rem), rem

            rem = rem_ref[q_rows, :]
            lax.while_loop(lambda c: (c[0] >= 0) & (c[1] > 0), body,
                           (i - j - 1, need_more(rem), rem))
            return 0

        lax.fori_loop(j + 1, nb, query_block, 0)

    for j, least in least_rem.items():
        @pl.when(jnp.min(least) < SB_ZERO_WEIGHT)
        def _():
            continue_rows(j)

    o_ref[0] = acc_ref[...].astype(BF16)


def _sb_attn(q, k, v):
    b, s, d = q.shape
    pairs = d // LANES
    blockspec = pl.BlockSpec((1, s, LANES), lambda i, j: (i, 0, j))
    return pl.pallas_call(
        _sb_attn_body,
        out_shape=jax.ShapeDtypeStruct((b, s, d), BF16),
        grid=(b, pairs),
        in_specs=[blockspec, blockspec, blockspec],
        out_specs=blockspec,
        scratch_shapes=[pltpu.VMEM((s, LANES), BF16)] * 4 + [pltpu.VMEM((s, LANES), F32)] * 2,
        compiler_params=_params(2),
        name="sb_attn",
    )(q, k, v)


def _sb_mixer(x, batch, attn_g, w_in, layer, gq, gk):
    t, d = x.shape
    reps = LANES // SB_HEAD_DIM
    q, k, v = _sb_inproj(x, attn_g, w_in, layer,
                         jnp.tile(gq, reps)[None, :], jnp.tile(gk, reps)[None, :])
    shape3 = (batch, t // batch, d)
    o = _sb_attn(q.reshape(shape3), k.reshape(shape3), v.reshape(shape3))
    return o.reshape(t, d)


def _gla_inproj_body(x_ref, g_ref, w_ref, wg1_ref, wg2_ref, bg_ref,
                     q_ref, k_ref, v_ref, r_ref, la_ref, *, kd, vd, q_scale):
    h = _rmsnorm(x_ref[...], g_ref[...]).astype(BF16)
    low_rank = jnp.dot(h, wg1_ref[...], preferred_element_type=F32).astype(BF16)
    gate = jnp.dot(low_rank, wg2_ref[...], preferred_element_type=F32) + bg_ref[...]
    softplus_neg = jnp.maximum(-gate, 0.0) + jnp.log(1.0 + jnp.exp(-jnp.abs(gate)))
    la_ref[...] = -softplus_neg / GLA_TAU
    proj = jnp.dot(h, w_ref[...], preferred_element_type=F32)
    q_ref[...] = (proj[:, :kd] * q_scale).astype(BF16)
    k_ref[...] = proj[:, kd:2 * kd].astype(BF16)
    v_ref[...] = proj[:, 2 * kd:2 * kd + vd].astype(BF16)
    r_ref[...] = proj[:, 2 * kd + vd:].astype(BF16)


def _gla_inproj(x, g, w, layer, wg1, wg2, bg, kd, vd):
    t, d = x.shape
    body = functools.partial(_gla_inproj_body, kd=kd, vd=vd,
                             q_scale=(kd // GLA_HEADS) ** -0.5)
    return pl.pallas_call(
        body,
        out_shape=(jax.ShapeDtypeStruct((t, kd), BF16), jax.ShapeDtypeStruct((t, kd), BF16),
                   jax.ShapeDtypeStruct((t, vd), BF16), jax.ShapeDtypeStruct((t, vd), BF16),
                   jax.ShapeDtypeStruct((t, kd), F32)),
        grid=(t // INPROJ_ROW_TILE,),
        in_specs=[_rows(d, INPROJ_ROW_TILE), _resident((1, d)), _layer_of(w, layer, 2 * kd + 2 * vd),
                  _resident(wg1.shape), _resident(wg2.shape), _resident((1, kd))],
        out_specs=tuple(_rows(width, INPROJ_ROW_TILE) for width in (kd, kd, vd, vd, kd)),
        compiler_params=_params(1),
        name="gla_inproj",
    )(x, g, w, wg1, wg2, bg)


def _gla_scan_body(q_ref, k_ref, la_ref, v_ref, o_ref, rowbuf_ref):
    hk = q_ref.shape[2] // GLA_HEADS_PER_STEP
    hv = v_ref.shape[2] // GLA_HEADS_PER_STEP
    for h in range(GLA_HEADS_PER_STEP):
        _gla_scan_head(q_ref, k_ref, la_ref, v_ref, o_ref, rowbuf_ref,
                       slice(h * hk, (h + 1) * hk), slice(h * hv, (h + 1) * hv))


def _gla_scan_head(q_ref, k_ref, la_ref, v_ref, o_ref, rowbuf_ref, key_lanes, value_lanes):
    seq = q_ref.shape[1]
    c = GLA_CHUNK
    row = lax.broadcasted_iota(jnp.int32, (c, c), 0)
    col = lax.broadcasted_iota(jnp.int32, (c, c), 1)
    causal = col <= row
    prefix = causal.astype(BF16)
    chunks = [slice(n * c, (n + 1) * c) for n in range(seq // c)]
    hk = key_lanes.stop - key_lanes.start

    def chunk_cum(la):
        la_hi, la_lo = _split_bf16(la)
        both = jnp.dot(prefix, jnp.concatenate([la_hi, la_lo], axis=1), preferred_element_type=F32)
        return both[:, :hk] + both[:, hk:]

    def factored_scores(qd, kf, cum):
        kd = (kf * jnp.exp(jnp.minimum(-cum, GLA_SAFE_DECAY))).astype(BF16)
        scores = lax.dot_general(qd, kd, _NT, preferred_element_type=F32)
        return jnp.where(causal, scores, 0.0).astype(BF16)

    cums = [chunk_cum(la_ref[0, rows, key_lanes]) for rows in chunks]
    qds, k_ends, decays, intra = [], [], [], []
    lowest = cums[0][c - 1:c, :]
    for rows, cum in zip(chunks, cums):
        qf = q_ref[0, rows, key_lanes].astype(F32)
        kf = k_ref[0, rows, key_lanes].astype(F32)
        last = cum[c - 1:c, :]
        lowest = jnp.minimum(lowest, last)
        qd = (qf * jnp.exp(cum)).astype(BF16)
        qds.append(qd)
        k_ends.append((kf * jnp.exp(last - cum)).astype(BF16))
        decays.append(jnp.exp(last))
        intra.append(jnp.dot(factored_scores(qd, kf, cum), v_ref[0, rows, value_lanes],
                             preferred_element_type=F32))
    updates = [lax.dot_general(v_ref[0, rows, value_lanes], k_end, _TN, preferred_element_type=F32)
               for rows, k_end in zip(chunks, k_ends)]
    state = None
    for n, rows in enumerate(chunks):
        out = intra[n]
        if state is not None:
            out = out + lax.dot_general(qds[n], state.astype(BF16), _NT, preferred_element_type=F32)
        o_ref[0, rows, value_lanes] = out.astype(o_ref.dtype)
        state = updates[n] if state is None else state * decays[n] + updates[n]

    def redo_chunk(n, _):
        rows = pl.ds(pl.multiple_of(n * c, c), c)
        cum = chunk_cum(la_ref[0, rows, key_lanes])

        @pl.when(jnp.min(cum[c - 1:c, :]) < -GLA_SAFE_DECAY)
        def _():
            qf = q_ref[0, rows, key_lanes].astype(F32)
            kf = k_ref[0, rows, key_lanes].astype(F32)
            vv = v_ref[0, rows, value_lanes]
            rowbuf_ref[0] = kf
            rowbuf_ref[1] = cum

            def key_column(s, scores):
                k_s = rowbuf_ref[0, pl.ds(s, 1), :]
                cum_s = rowbuf_ref[1, pl.ds(s, 1), :]
                terms = qf * jnp.exp(jnp.minimum(cum - cum_s, 0.0)) * k_s
                column = jnp.sum(terms, axis=-1, keepdims=True)
                return jnp.where((col == s) & (row >= s), column, scores)

            exact = lax.fori_loop(0, c, key_column, jnp.zeros((c, c), F32)).astype(BF16)
            factored = factored_scores((qf * jnp.exp(cum)).astype(BF16), kf, cum)
            correction = (jnp.dot(exact, vv, preferred_element_type=F32)
                          - jnp.dot(factored, vv, preferred_element_type=F32))
            o_ref[0, rows, value_lanes] = (o_ref[0, rows, value_lanes].astype(F32)
                                           + correction).astype(o_ref.dtype)
        return 0

    @pl.when(jnp.min(lowest) < -GLA_SAFE_DECAY)
    def _():
        lax.fori_loop(0, len(chunks), redo_chunk, 0)


def _gla_scan(q, k, la, v):
    b, s, kd = q.shape
    vd = v.shape[2]
    steps = GLA_HEADS // GLA_HEADS_PER_STEP
    key_spec = pl.BlockSpec((1, s, kd // steps), lambda i, j: (i, 0, j))
    val_spec = pl.BlockSpec((1, s, vd // steps), lambda i, j: (i, 0, j))
    return pl.pallas_call(
        _gla_scan_body,
        out_shape=jax.ShapeDtypeStruct((b, s, vd), BF16),
        grid=(b, steps),
        in_specs=[key_spec, key_spec, key_spec, val_spec],
        out_specs=val_spec,
        scratch_shapes=[pltpu.VMEM((2, GLA_CHUNK, kd // GLA_HEADS), F32)],
        compiler_params=_params(2),
        name="gla_scan",
    )(q, k, la, v)


def _gla_mixer(x, batch, attn_g, w_in, layer, w_gate2, b_gate, vd):
    t, d = x.shape
    kd = w_gate2.shape[1]
    main = 2 * kd + 2 * vd
    wg1 = jnp.pad(w_in[layer, :, main:], ((0, 0), (0, LANES - GLA_GATE_RANK)))
    wg2 = jnp.pad(w_gate2, ((0, LANES - GLA_GATE_RANK), (0, 0)))
    q, k, v, r, la = _gla_inproj(x, attn_g, w_in, layer, wg1, wg2, b_gate[None, :], kd, vd)
    s = t // batch
    o = _gla_scan(q.reshape(batch, s, kd), k.reshape(batch, s, kd), la.reshape(batch, s, kd),
                  v.reshape(batch, s, vd))
    return o.reshape(t, vd), r


def kernel(x, attn_norm_g, ffn_norm_g, gla_w_in, gla_w_gate2, gla_b_gate, gla_norm_g, gla_w_out,
           sb_w_in, sb_q_norm_g, sb_k_norm_g, sb_w_out, ffn_w_gate_up, ffn_w_down):
    batch, seq, d = x.shape
    depth = attn_norm_g.shape[0]
    assert (batch * seq) % ROW_TILE == 0 and (batch * seq) % INPROJ_ROW_TILE == 0
    assert seq % GLA_CHUNK == 0 and seq % SB_BLOCK == 0
    gla_w_in, gla_w_gate2, gla_w_out, sb_w_in, sb_w_out, ffn_w_gate_up, ffn_w_down = (
        w.astype(BF16) for w in (gla_w_in, gla_w_gate2, gla_w_out, sb_w_in, sb_w_out,
                                 ffn_w_gate_up, ffn_w_down))
    xt = x.reshape(batch * seq, d)
    for i in range(depth):
        j = i // 2
        attn_g = attn_norm_g[i][None, :]
        ffn = (ffn_norm_g[i][None, :], ffn_w_gate_up, ffn_w_down, i)
        if i % 2 == 0:
            o, r = _gla_mixer(xt, batch, attn_g, gla_w_in, j, gla_w_gate2[j], gla_b_gate[j],
                              gla_w_out.shape[1])
            xt = _gla_out_ffn(xt, o, r, gla_norm_g[j][None, :], gla_w_out, j, *ffn)
        else:
            o = _sb_mixer(xt, batch, attn_g, sb_w_in, j, sb_q_norm_g[j], sb_k_norm_g[j])
            xt = _sb_out_ffn(xt, o, sb_w_out, j, *ffn)
    return xt.reshape(batch, seq, d)
```

```python
import functools

import jax
import jax.numpy as jnp
from jax import lax
from jax.experimental import pallas as pl
from jax.experimental.pallas import tpu as pltpu

F32 = jnp.float32
BF16 = jnp.bfloat16

NORM_EPS = 1e-6
GLA_HEADS = 4
GLA_TAU = 16.0
GLA_GATE_RANK = 16
SB_HEAD_DIM = 64

LANES = 128
ROW_TILE = 512
INPROJ_ROW_TILE = 1024
GLA_CHUNK = 128
GLA_HEADS_PER_STEP = 2
GLA_SAFE_DECAY = 60.0
SB_BLOCK = LANES // 2
SB_PASS1_ROWS = (64, 64, 64, 32)
SB_GROUP = 2
BF16_SUBLANES = 16
SB_ZERO_WEIGHT = 150.0
LOG2E = 1.4426950408889634
VMEM_LIMIT = 56 * 1024 * 1024

_NT = (((1,), (1,)), ((), ()))
_TN = (((0,), (0,)), ((), ()))


def _rmsnorm(xf, g):
    return xf * lax.rsqrt(jnp.mean(xf * xf, axis=-1, keepdims=True) + NORM_EPS) * g


def _split_bf16(x):
    hi = x.astype(BF16)
    lo = (x - hi.astype(F32)).astype(BF16)
    return hi, lo


def _params(n_parallel):
    return pltpu.CompilerParams(
        dimension_semantics=("parallel",) * n_parallel,
        vmem_limit_bytes=VMEM_LIMIT,
    )


def _resident(shape):
    return pl.BlockSpec(shape, lambda *_: (0,) * len(shape), pipeline_mode=pl.Buffered(1))


def _layer_of(stacked, layer, width=None):
    shape = (stacked.shape[1], width or stacked.shape[2])
    return pl.BlockSpec((None,) + shape, lambda *_: (layer, 0, 0), pipeline_mode=pl.Buffered(1))


def _rows(width, tile=ROW_TILE):
    return pl.BlockSpec((tile, width), lambda i: (i, 0))


def _swiglu_residual(x, g_ref, wgu_ref, wd_ref):
    d_ff = wd_ref.shape[0]
    h = _rmsnorm(x, g_ref[...]).astype(BF16)
    gu = jnp.dot(h, wgu_ref[...], preferred_element_type=F32)
    gate = gu[:, :d_ff]
    up = gu[:, d_ff:]
    act = (gate * jax.nn.sigmoid(gate) * up).astype(BF16)
    return x + jnp.dot(act, wd_ref[...], preferred_element_type=F32)


def _sb_out_ffn_body(x_ref, o_ref, wo_ref, g_ref, wgu_ref, wd_ref, y_ref):
    x = x_ref[...] + jnp.dot(o_ref[...], wo_ref[...], preferred_element_type=F32)
    y_ref[...] = _swiglu_residual(x, g_ref, wgu_ref, wd_ref)


def _sb_out_ffn(x, o, w_out, mixer_layer, g, w_gate_up, w_down, layer):
    t, d = x.shape
    return pl.pallas_call(
        _sb_out_ffn_body,
        out_shape=jax.ShapeDtypeStruct((t, d), F32),
        grid=(t // ROW_TILE,),
        in_specs=[_rows(d), _rows(d), _layer_of(w_out, mixer_layer), _resident((1, d)),
                  _layer_of(w_gate_up, layer), _layer_of(w_down, layer)],
        out_specs=_rows(d),
        compiler_params=_params(1),
        name="sb_out_ffn",
    )(x, o, w_out, g, w_gate_up, w_down)


def _gla_out_ffn_body(x_ref, o_ref, r_ref, ng_ref, wo_ref, g_ref, wgu_ref, wd_ref, y_ref):
    hv = ng_ref.shape[1]
    o = o_ref[...].astype(F32)
    r = r_ref[...].astype(F32)
    ng = ng_ref[...]
    parts = []
    for h in range(o.shape[1] // hv):
        sl = slice(h * hv, (h + 1) * hv)
        rh = r[:, sl]
        parts.append((_rmsnorm(o[:, sl], ng) * (rh * jax.nn.sigmoid(rh))).astype(BF16))
    gated = jnp.concatenate(parts, axis=-1)
    x = x_ref[...] + jnp.dot(gated, wo_ref[...], preferred_element_type=F32)
    y_ref[...] = _swiglu_residual(x, g_ref, wgu_ref, wd_ref)


def _gla_out_ffn(x, o, r, ng, w_out, mixer_layer, g, w_gate_up, w_down, layer):
    t, d = x.shape
    vd = o.shape[1]
    return pl.pallas_call(
        _gla_out_ffn_body,
        out_shape=jax.ShapeDtypeStruct((t, d), F32),
        grid=(t // ROW_TILE,),
        in_specs=[_rows(d), _rows(vd), _rows(vd), _resident(ng.shape), _layer_of(w_out, mixer_layer),
                  _resident((1, d)), _layer_of(w_gate_up, layer), _layer_of(w_down, layer)],
        out_specs=_rows(d),
        compiler_params=_params(1),
        name="gla_out_ffn",
    )(x, o, r, ng, w_out, g, w_gate_up, w_down)


def _sb_inproj_body(x_ref, g_ref, w_ref, gq_ref, gk_ref, q_ref, k_ref, v_ref):
    d = x_ref.shape[1]
    h = _rmsnorm(x_ref[...], g_ref[...]).astype(BF16)
    qkv = jnp.dot(h, w_ref[...], preferred_element_type=F32)
    first = lax.broadcasted_iota(jnp.int32, (1, LANES), 1) < SB_HEAD_DIM

    def head_norm(t, gain):
        sq = t * t
        ss_a = jnp.sum(jnp.where(first, sq, 0.0), axis=-1, keepdims=True)
        ss_b = jnp.sum(jnp.where(first, 0.0, sq), axis=-1, keepdims=True)
        inv = jnp.where(first,
                        lax.rsqrt(ss_a / SB_HEAD_DIM + NORM_EPS),
                        lax.rsqrt(ss_b / SB_HEAD_DIM + NORM_EPS))
        return (t * inv * gain).astype(BF16)

    gq = gq_ref[...] * (SB_HEAD_DIM ** -0.5 * LOG2E)
    gk = gk_ref[...]
    for c in range(d // LANES):
        cols = slice(c * LANES, (c + 1) * LANES)
        q_ref[:, cols] = head_norm(qkv[:, cols], gq)
        k_ref[:, cols] = head_norm(qkv[:, d + c * LANES:d + (c + 1) * LANES], gk)
    v_ref[...] = qkv[:, 2 * d:].astype(BF16)


def _sb_inproj(x, g, w, layer, gq, gk):
    t, d = x.shape
    out = jax.ShapeDtypeStruct((t, d), BF16)
    return pl.pallas_call(
        _sb_inproj_body,
        out_shape=(out, out, out),
        grid=(t // INPROJ_ROW_TILE,),
        in_specs=[_rows(d, INPROJ_ROW_TILE), _resident((1, d)), _layer_of(w, layer),
                  _resident((1, LANES)), _resident((1, LANES))],
        out_specs=(_rows(d, INPROJ_ROW_TILE),) * 3,
        compiler_params=_params(1),
        name="sb_inproj",
    )(x, g, w, gq, gk)


def _softplus_bits(z):
    return jnp.maximum(z, 0.0) + jnp.log(1.0 + jnp.exp2(-jnp.abs(z))) * LOG2E


def _sb_attn_body(q_ref, k_ref, v_ref, o_ref, ka_ref, kb_ref, va_ref, vb_ref, acc_ref, rem_ref):
    seq = q_ref.shape[1]
    blk = SB_BLOCK
    nb = seq // blk
    first = lax.broadcasted_iota(jnp.int32, (1, LANES), 1) < SB_HEAD_DIM
    kk = k_ref[0]
    vv = v_ref[0]
    zero = jnp.zeros_like(kk)
    ka_ref[...] = jnp.where(first, kk, zero)
    kb_ref[...] = jnp.where(first, zero, kk)
    va_ref[...] = jnp.where(first, vv, zero)
    vb_ref[...] = jnp.where(first, zero, vv)

    row = lax.broadcasted_iota(jnp.int32, (blk, LANES), 0)
    col = lax.broadcasted_iota(jnp.int32, (blk, LANES), 1)
    strict = (col & (blk - 1)) < row
    src = lax.broadcasted_iota(jnp.int32, (LANES, 2 * LANES), 0)
    dst = lax.broadcasted_iota(jnp.int32, (LANES, 2 * LANES), 1)
    same_head = (src & blk) == (dst & blk)
    later_and_total = (same_head & ((dst >= LANES) | ((src & (blk - 1)) > (dst & (blk - 1))))
                       ).astype(BF16)

    def keys_values(keys):
        return (jnp.concatenate([ka_ref[keys, :], kb_ref[keys, :]], axis=0),
                jnp.concatenate([va_ref[keys, :], vb_ref[keys, :]], axis=0))

    def mask_first_block(x):
        masked = jnp.where(strict, x[:blk], 0.0)
        return masked if x.shape[0] == blk else jnp.concatenate([masked, x[blk:]], axis=0)

    def betas_stage(z, diagonal):
        sp = _softplus_bits(z)
        spm = mask_first_block(sp) if diagonal else sp
        return z - sp, spm.astype(BF16)

    def sums_stage(spm):
        return jnp.dot(spm, later_and_total, preferred_element_type=F32)

    def weights_stage(log_beta, sums, rem, diagonal):
        w = jnp.exp2(log_beta - sums[:, :LANES] - rem)
        return (mask_first_block(w) if diagonal else w).astype(BF16)

    def block_step(q_rows, k2, v2, rem):
        z = lax.dot_general(q_rows, k2, _NT, preferred_element_type=F32)
        log_beta, spm = betas_stage(z, False)
        sums = sums_stage(spm)
        w = weights_stage(log_beta, sums, rem, False)
        return jnp.dot(w, v2, preferred_element_type=F32), sums[:, LANES:]

    span_lo = SB_PASS1_ROWS[1:] + (0,)
    carried = {}
    least_rem = {}
    keys_of = lambda kb: slice(kb * blk, (kb + 1) * blk)

    def rows_of(kb):
        offsets = [j for j in range(len(SB_PASS1_ROWS)) if kb + j < nb]
        return slice(kb * blk, (kb + offsets[-1]) * blk + SB_PASS1_ROWS[offsets[-1]]), offsets

    def pair_scores(kb):
        union = slice((kb - 1) * blk, rows_of(kb)[0].stop)
        (k_hi, v_hi), (k_lo, v_lo) = keys_values(keys_of(kb)), keys_values(keys_of(kb - 1))
        z = lax.dot_general(q_ref[0, union, :], jnp.concatenate([k_hi, k_lo], axis=0), _NT,
                            preferred_element_type=F32)
        n_lo = rows_of(kb - 1)[0].stop - union.start
        betas = [betas_stage(z[blk:, :LANES], True), betas_stage(z[:n_lo, LANES:], True)]
        return betas, jnp.concatenate([v_hi, v_lo], axis=0)

    def block_weights(kb, log_beta, unit_sums):
        _, offsets = rows_of(kb)
        rem = jnp.concatenate([jnp.zeros((blk, LANES), F32)] + [carried[j] for j in offsets[1:]],
                              axis=0)
        after = rem + unit_sums[:, LANES:]
        for j in offsets:
            rem_j = after[j * blk:j * blk + SB_PASS1_ROWS[j]]
            if j + 1 < len(SB_PASS1_ROWS):
                carried[j + 1] = rem_j[:SB_PASS1_ROWS[j + 1]]
            if kb > 0 and span_lo[j] < SB_PASS1_ROWS[j]:
                done = rem_j[span_lo[j]:]
                rem_ref[(kb + j) * blk + span_lo[j]:(kb + j) * blk + SB_PASS1_ROWS[j], :] = done
                least_rem[j] = jnp.minimum(least_rem[j], done) if j in least_rem else done
        return weights_stage(log_beta, unit_sums, rem, True)

    pairs = list(range(nb - 1, 0, -2))
    for g0 in range(0, len(pairs), SB_GROUP):
        group = pairs[g0:g0 + SB_GROUP]
        staged = [pair_scores(kb) for kb in group]
        sums = [[sums_stage(spm) for _, spm in betas] for betas, _ in staged]
        for kb, (betas, v4), pair_sums in zip(group, staged, sums):
            w_hi = block_weights(kb, betas[0][0], pair_sums[0])
            w_lo = block_weights(kb - 1, betas[1][0], pair_sums[1])
            n_union = w_hi.shape[0] + blk
            pad_lo = n_union - w_lo.shape[0]
            if pad_lo:
                w_lo = jnp.concatenate([w_lo, jnp.zeros((pad_lo, LANES), BF16)], axis=0)
            w = jnp.concatenate(
                [jnp.concatenate([jnp.zeros((blk, LANES), BF16), w_hi], axis=0), w_lo], axis=1)
            out = jnp.dot(w, v4, preferred_element_type=F32)
            lo = (kb - 1) * blk
            acc_ref[lo:lo + 2 * blk, :] = out[:2 * blk]
            if n_union > 2 * blk:
                acc_ref[lo + 2 * blk:lo + n_union, :] += out[2 * blk:]

    def need_more(rem):
        return (jnp.min(rem) < SB_ZERO_WEIGHT).astype(jnp.int32)

    def continue_rows(j):
        n_rows = SB_PASS1_ROWS[j] - span_lo[j]

        def query_block(i, _):
            q_rows = pl.ds(pl.multiple_of(i * blk + span_lo[j], BF16_SUBLANES), n_rows)

            def body(c):
                kb, _, rem = c
                k2, v2 = keys_values(pl.ds(pl.multiple_of(kb * blk, blk), blk))
                out, total = block_step(q_ref[0, q_rows, :], k2, v2, rem)
                acc_ref[q_rows, :] += out
                rem = rem + total
                return kb - 1, need_more(rem), rem

            rem = rem_ref[q_rows, :]
            lax.while_loop(lambda c: (c[0] >= 0) & (c[1] > 0), body,
                           (i - j - 1, need_more(rem), rem))
            return 0

        lax.fori_loop(j + 1, nb, query_block, 0)

    for j, least in least_rem.items():
        @pl.when(jnp.min(least) < SB_ZERO_WEIGHT)
        def _():
            continue_rows(j)

    o_ref[0] = acc_ref[...].astype(BF16)


def _sb_attn(q, k, v):
    b, s, d = q.shape
    pairs = d // LANES
    blockspec = pl.BlockSpec((1, s, LANES), lambda i, j: (i, 0, j))
    return pl.pallas_call(
        _sb_attn_body,
        out_shape=jax.ShapeDtypeStruct((b, s, d), BF16),
        grid=(b, pairs),
        in_specs=[blockspec, blockspec, blockspec],
        out_specs=blockspec,
        scratch_shapes=[pltpu.VMEM((s, LANES), BF16)] * 4 + [pltpu.VMEM((s, LANES), F32)] * 2,
        compiler_params=_params(2),
        name="sb_attn",
    )(q, k, v)


def _sb_mixer(x, batch, attn_g, w_in, layer, gq, gk):
    t, d = x.shape
    reps = LANES // SB_HEAD_DIM
    q, k, v = _sb_inproj(x, attn_g, w_in, layer,
                         jnp.tile(gq, reps)[None, :], jnp.tile(gk, reps)[None, :])
    shape3 = (batch, t // batch, d)
    o = _sb_attn(q.reshape(shape3), k.reshape(shape3), v.reshape(shape3))
    return o.reshape(t, d)


def _gla_inproj_body(x_ref, g_ref, w_ref, wg1_ref, wg2_ref, bg_ref,
                     q_ref, k_ref, v_ref, r_ref, la_ref, *, kd, vd, q_scale):
    h = _rmsnorm(x_ref[...], g_ref[...]).astype(BF16)
    low_rank = jnp.dot(h, wg1_ref[...], preferred_element_type=F32).astype(BF16)
    gate = jnp.dot(low_rank, wg2_ref[...], preferred_element_type=F32) + bg_ref[...]
    softplus_neg = jnp.maximum(-gate, 0.0) + jnp.log(1.0 + jnp.exp(-jnp.abs(gate)))
    la_ref[...] = -softplus_neg / GLA_TAU
    proj = jnp.dot(h, w_ref[...], preferred_element_type=F32)
    q_ref[...] = (proj[:, :kd] * q_scale).astype(BF16)
    k_ref[...] = proj[:, kd:2 * kd].astype(BF16)
    v_ref[...] = proj[:, 2 * kd:2 * kd + vd].astype(BF16)
    r_ref[...] = proj[:, 2 * kd + vd:].astype(BF16)


def _gla_inproj(x, g, w, layer, wg1, wg2, bg, kd, vd):
    t, d = x.shape
    body = functools.partial(_gla_inproj_body, kd=kd, vd=vd,
                             q_scale=(kd // GLA_HEADS) ** -0.5)
    return pl.pallas_call(
        body,
        out_shape=(jax.ShapeDtypeStruct((t, kd), BF16), jax.ShapeDtypeStruct((t, kd), BF16),
                   jax.ShapeDtypeStruct((t, vd), BF16), jax.ShapeDtypeStruct((t, vd), BF16),
                   jax.ShapeDtypeStruct((t, kd), F32)),
        grid=(t // INPROJ_ROW_TILE,),
        in_specs=[_rows(d, INPROJ_ROW_TILE), _resident((1, d)), _layer_of(w, layer, 2 * kd + 2 * vd),
                  _resident(wg1.shape), _resident(wg2.shape), _resident((1, kd))],
        out_specs=tuple(_rows(width, INPROJ_ROW_TILE) for width in (kd, kd, vd, vd, kd)),
        compiler_params=_params(1),
        name="gla_inproj",
    )(x, g, w, wg1, wg2, bg)


def _gla_scan_body(q_ref, k_ref, la_ref, v_ref, o_ref, rowbuf_ref):
    hk = q_ref.shape[2] // GLA_HEADS_PER_STEP
    hv = v_ref.shape[2] // GLA_HEADS_PER_STEP
    for h in range(GLA_HEADS_PER_STEP):
        _gla_scan_head(q_ref, k_ref, la_ref, v_ref, o_ref, rowbuf_ref,
                       slice(h * hk, (h + 1) * hk), slice(h * hv, (h + 1) * hv))


def _gla_scan_head(q_ref, k_ref, la_ref, v_ref, o_ref, rowbuf_ref, key_lanes, value_lanes):
    seq = q_ref.shape[1]
    c = GLA_CHUNK
    row = lax.broadcasted_iota(jnp.int32, (c, c), 0)
    col = lax.broadcasted_iota(jnp.int32, (c, c), 1)
    causal = col <= row
    prefix = causal.astype(BF16)
    chunks = [slice(n * c, (n + 1) * c) for n in range(seq // c)]
    hk = key_lanes.stop - key_lanes.start

    def chunk_cum(la):
        la_hi, la_lo = _split_bf16(la)
        both = jnp.dot(prefix, jnp.concatenate([la_hi, la_lo], axis=1), preferred_element_type=F32)
        return both[:, :hk] + both[:, hk:]

    def factored_scores(qd, kf, cum):
        kd = (kf * jnp.exp(jnp.minimum(-cum, GLA_SAFE_DECAY))).astype(BF16)
        scores = lax.dot_general(qd, kd, _NT, preferred_element_type=F32)
        return jnp.where(causal, scores, 0.0).astype(BF16)

    cums = [chunk_cum(la_ref[0, rows, key_lanes]) for rows in chunks]
    qds, k_ends, decays, intra = [], [], [], []
    lowest = cums[0][c - 1:c, :]
    for rows, cum in zip(chunks, cums):
        qf = q_ref[0, rows, key_lanes].astype(F32)
        kf = k_ref[0, rows, key_lanes].astype(F32)
        last = cum[c - 1:c, :]
        lowest = jnp.minimum(lowest, last)
        qd = (qf * jnp.exp(cum)).astype(BF16)
        qds.append(qd)
        k_ends.append((kf * jnp.exp(last - cum)).astype(BF16))
        decays.append(jnp.exp(last))
        intra.append(jnp.dot(factored_scores(qd, kf, cum), v_ref[0, rows, value_lanes],
                             preferred_element_type=F32))
    updates = [lax.dot_general(v_ref[0, rows, value_lanes], k_end, _TN, preferred_element_type=F32)
               for rows, k_end in zip(chunks, k_ends)]
    state = None
    for n, rows in enumerate(chunks):
        out = intra[n]
        if state is not None:
            out = out + lax.dot_general(qds[n], state.astype(BF16), _NT, preferred_element_type=F32)
        o_ref[0, rows, value_lanes] = out.astype(o_ref.dtype)
        state = updates[n] if state is None else state * decays[n] + updates[n]

    def redo_chunk(n, _):
        rows = pl.ds(pl.multiple_of(n * c, c), c)
        cum = chunk_cum(la_ref[0, rows, key_lanes])

        @pl.when(jnp.min(cum[c - 1:c, :]) < -GLA_SAFE_DECAY)
        def _():
            qf = q_ref[0, rows, key_lanes].astype(F32)
            kf = k_ref[0, rows, key_lanes].astype(F32)
            vv = v_ref[0, rows, value_lanes]
            rowbuf_ref[0] = kf
            rowbuf_ref[1] = cum

            def key_column(s, scores):
                k_s = rowbuf_ref[0, pl.ds(s, 1), :]
                cum_s = rowbuf_ref[1, pl.ds(s, 1), :]
                terms = qf * jnp.exp(jnp.minimum(cum - cum_s, 0.0)) * k_s
                column = jnp.sum(terms, axis=-1, keepdims=True)
                return jnp.where((col == s) & (row >= s), column, scores)

            exact = lax.fori_loop(0, c, key_column, jnp.zeros((c, c), F32)).astype(BF16)
            factored = factored_scores((qf * jnp.exp(cum)).astype(BF16), kf, cum)
            correction = (jnp.dot(exact, vv, preferred_element_type=F32)
                          - jnp.dot(factored, vv, preferred_element_type=F32))
            o_ref[0, rows, value_lanes] = (o_ref[0, rows, value_lanes].astype(F32)
                                           + correction).astype(o_ref.dtype)
        return 0

    @pl.when(jnp.min(lowest) < -GLA_SAFE_DECAY)
    def _():
        lax.fori_loop(0, len(chunks), redo_chunk, 0)


def _gla_scan(q, k, la, v):
    b, s, kd = q.shape
    vd = v.shape[2]
    steps = GLA_HEADS // GLA_HEADS_PER_STEP
    key_spec = pl.BlockSpec((1, s, kd // steps), lambda i, j: (i, 0, j))
    val_spec = pl.BlockSpec((1, s, vd // steps), lambda i, j: (i, 0, j))
    return pl.pallas_call(
        _gla_scan_body,
        out_shape=jax.ShapeDtypeStruct((b, s, vd), BF16),
        grid=(b, steps),
        in_specs=[key_spec, key_spec, key_spec, val_spec],
        out_specs=val_spec,
        scratch_shapes=[pltpu.VMEM((2, GLA_CHUNK, kd // GLA_HEADS), F32)],
        compiler_params=_params(2),
        name="gla_scan",
    )(q, k, la, v)


def _gla_mixer(x, batch, attn_g, w_in, layer, w_gate2, b_gate, vd):
    t, d = x.shape
    kd = w_gate2.shape[1]
    main = 2 * kd + 2 * vd
    wg1 = jnp.pad(w_in[layer, :, main:], ((0, 0), (0, LANES - GLA_GATE_RANK)))
    wg2 = jnp.pad(w_gate2, ((0, LANES - GLA_GATE_RANK), (0, 0)))
    q, k, v, r, la = _gla_inproj(x, attn_g, w_in, layer, wg1, wg2, b_gate[None, :], kd, vd)
    s = t // batch
    o = _gla_scan(q.reshape(batch, s, kd), k.reshape(batch, s, kd), la.reshape(batch, s, kd),
                  v.reshape(batch, s, vd))
    return o.reshape(t, vd), r


def kernel(x, attn_norm_g, ffn_norm_g, gla_w_in, gla_w_gate2, gla_b_gate, gla_norm_g, gla_w_out,
           sb_w_in, sb_q_norm_g, sb_k_norm_g, sb_w_out, ffn_w_gate_up, ffn_w_down):
    batch, seq, d = x.shape
    depth = attn_norm_g.shape[0]
    assert (batch * seq) % ROW_TILE == 0 and (batch * seq) % INPROJ_ROW_TILE == 0
    assert seq % GLA_CHUNK == 0 and seq % SB_BLOCK == 0
    gla_w_in, gla_w_gate2, gla_w_out, sb_w_in, sb_w_out, ffn_w_gate_up, ffn_w_down = (
        w.astype(BF16) for w in (gla_w_in, gla_w_gate2, gla_w_out, sb_w_in, sb_w_out,
                                 ffn_w_gate_up, ffn_w_down))
    xt = x.reshape(batch * seq, d)
    for i in range(depth):
        j = i // 2
        attn_g = attn_norm_g[i][None, :]
        ffn = (ffn_norm_g[i][None, :], ffn_w_gate_up, ffn_w_down, i)
        if i % 2 == 0:
            o, r = _gla_mixer(xt, batch, attn_g, gla_w_in, j, gla_w_gate2[j], gla_b_gate[j],
                              gla_w_out.shape[1])
            xt = _gla_out_ffn(xt, o, r, gla_norm_g[j][None, :], gla_w_out, j, *ffn)
        else:
            o = _sb_mixer(xt, batch, attn_g, sb_w_in, j, sb_q_norm_g[j], sb_k_norm_g[j])
            xt = _sb_out_ffn(xt, o, sb_w_out, j, *ffn)
    return xt.reshape(batch, seq, d)
```

```python
import functools

import jax
import jax.numpy as jnp
from jax import lax
from jax.experimental import pallas as pl
from jax.experimental.pallas import tpu as pltpu

F32 = jnp.float32
BF16 = jnp.bfloat16

NORM_EPS = 1e-6
GLA_HEADS = 4
GLA_TAU = 16.0
GLA_GATE_RANK = 16
SB_HEAD_DIM = 64

LANES = 128
ROW_TILE = 512
INPROJ_ROW_TILE = 1024
GLA_CHUNK = 128
GLA_HEADS_PER_STEP = 2
GLA_SAFE_DECAY = 60.0
SB_BLOCK = LANES // 2
SB_PASS1_ROWS = (64, 64, 64, 32)
SB_GROUP = 4
BF16_SUBLANES = 16
SB_ZERO_WEIGHT = 150.0
LOG2E = 1.4426950408889634
VMEM_LIMIT = 56 * 1024 * 1024

_NT = (((1,), (1,)), ((), ()))
_TN = (((0,), (0,)), ((), ()))


def _rmsnorm(xf, g):
    return xf * lax.rsqrt(jnp.mean(xf * xf, axis=-1, keepdims=True) + NORM_EPS) * g


def _split_bf16(x):
    hi = x.astype(BF16)
    lo = (x - hi.astype(F32)).astype(BF16)
    return hi, lo


def _params(n_parallel):
    return pltpu.CompilerParams(
        dimension_semantics=("parallel",) * n_parallel,
        vmem_limit_bytes=VMEM_LIMIT,
    )


def _resident(shape):
    return pl.BlockSpec(shape, lambda *_: (0,) * len(shape), pipeline_mode=pl.Buffered(1))


def _layer_of(stacked, layer, width=None):
    shape = (stacked.shape[1], width or stacked.shape[2])
    return pl.BlockSpec((None,) + shape, lambda *_: (layer, 0, 0), pipeline_mode=pl.Buffered(1))


def _rows(width, tile=ROW_TILE):
    return pl.BlockSpec((tile, width), lambda i: (i, 0))


def _swiglu_residual(x, g_ref, wgu_ref, wd_ref):
    d_ff = wd_ref.shape[0]
    h = _rmsnorm(x, g_ref[...]).astype(BF16)
    gu = jnp.dot(h, wgu_ref[...], preferred_element_type=F32)
    gate = gu[:, :d_ff]
    up = gu[:, d_ff:]
    act = (gate * jax.nn.sigmoid(gate) * up).astype(BF16)
    return x + jnp.dot(act, wd_ref[...], preferred_element_type=F32)


def _sb_out_ffn_body(x_ref, o_ref, wo_ref, g_ref, wgu_ref, wd_ref, y_ref):
    x = x_ref[...] + jnp.dot(o_ref[...], wo_ref[...], preferred_element_type=F32)
    y_ref[...] = _swiglu_residual(x, g_ref, wgu_ref, wd_ref)


def _sb_out_ffn(x, o, w_out, mixer_layer, g, w_gate_up, w_down, layer):
    t, d = x.shape
    return pl.pallas_call(
        _sb_out_ffn_body,
        out_shape=jax.ShapeDtypeStruct((t, d), F32),
        grid=(t // ROW_TILE,),
        in_specs=[_rows(d), _rows(d), _layer_of(w_out, mixer_layer), _resident((1, d)),
                  _layer_of(w_gate_up, layer), _layer_of(w_down, layer)],
        out_specs=_rows(d),
        compiler_params=_params(1),
        name="sb_out_ffn",
    )(x, o, w_out, g, w_gate_up, w_down)


def _gla_out_ffn_body(x_ref, o_ref, r_ref, ng_ref, wo_ref, g_ref, wgu_ref, wd_ref, y_ref):
    hv = ng_ref.shape[1]
    o = o_ref[...].astype(F32)
    r = r_ref[...].astype(F32)
    ng = ng_ref[...]
    parts = []
    for h in range(o.shape[1] // hv):
        sl = slice(h * hv, (h + 1) * hv)
        rh = r[:, sl]
        parts.append((_rmsnorm(o[:, sl], ng) * (rh * jax.nn.sigmoid(rh))).astype(BF16))
    gated = jnp.concatenate(parts, axis=-1)
    x = x_ref[...] + jnp.dot(gated, wo_ref[...], preferred_element_type=F32)
    y_ref[...] = _swiglu_residual(x, g_ref, wgu_ref, wd_ref)


def _gla_out_ffn(x, o, r, ng, w_out, mixer_layer, g, w_gate_up, w_down, layer):
    t, d = x.shape
    vd = o.shape[1]
    return pl.pallas_call(
        _gla_out_ffn_body,
        out_shape=jax.ShapeDtypeStruct((t, d), F32),
        grid=(t // ROW_TILE,),
        in_specs=[_rows(d), _rows(vd), _rows(vd), _resident(ng.shape), _layer_of(w_out, mixer_layer),
                  _resident((1, d)), _layer_of(w_gate_up, layer), _layer_of(w_down, layer)],
        out_specs=_rows(d),
        compiler_params=_params(1),
        name="gla_out_ffn",
    )(x, o, r, ng, w_out, g, w_gate_up, w_down)


def _sb_inproj_body(x_ref, g_ref, w_ref, gq_ref, gk_ref, q_ref, k_ref, v_ref):
    d = x_ref.shape[1]
    h = _rmsnorm(x_ref[...], g_ref[...]).astype(BF16)
    qkv = jnp.dot(h, w_ref[...], preferred_element_type=F32)
    first = lax.broadcasted_iota(jnp.int32, (1, LANES), 1) < SB_HEAD_DIM

    def head_norm(t, gain):
        sq = t * t
        ss_a = jnp.sum(jnp.where(first, sq, 0.0), axis=-1, keepdims=True)
        ss_b = jnp.sum(jnp.where(first, 0.0, sq), axis=-1, keepdims=True)
        inv = jnp.where(first,
                        lax.rsqrt(ss_a / SB_HEAD_DIM + NORM_EPS),
                        lax.rsqrt(ss_b / SB_HEAD_DIM + NORM_EPS))
        return (t * inv * gain).astype(BF16)

    gq = gq_ref[...] * (SB_HEAD_DIM ** -0.5 * LOG2E)
    gk = gk_ref[...]
    for c in range(d // LANES):
        cols = slice(c * LANES, (c + 1) * LANES)
        q_ref[:, cols] = head_norm(qkv[:, cols], gq)
        k_ref[:, cols] = head_norm(qkv[:, d + c * LANES:d + (c + 1) * LANES], gk)
    v_ref[...] = qkv[:, 2 * d:].astype(BF16)


def _sb_inproj(x, g, w, layer, gq, gk):
    t, d = x.shape
    out = jax.ShapeDtypeStruct((t, d), BF16)
    return pl.pallas_call(
        _sb_inproj_body,
        out_shape=(out, out, out),
        grid=(t // INPROJ_ROW_TILE,),
        in_specs=[_rows(d, INPROJ_ROW_TILE), _resident((1, d)), _layer_of(w, layer),
                  _resident((1, LANES)), _resident((1, LANES))],
        out_specs=(_rows(d, INPROJ_ROW_TILE),) * 3,
        compiler_params=_params(1),
        name="sb_inproj",
    )(x, g, w, gq, gk)


def _softplus_bits(z):
    return jnp.maximum(z, 0.0) + jnp.log(1.0 + jnp.exp2(-jnp.abs(z))) * LOG2E


def _sb_attn_body(q_ref, k_ref, v_ref, o_ref, ka_ref, kb_ref, va_ref, vb_ref, acc_ref, rem_ref):
    seq = q_ref.shape[1]
    blk = SB_BLOCK
    nb = seq // blk
    first = lax.broadcasted_iota(jnp.int32, (1, LANES), 1) < SB_HEAD_DIM
    kk = k_ref[0]
    vv = v_ref[0]
    zero = jnp.zeros_like(kk)
    ka_ref[...] = jnp.where(first, kk, zero)
    kb_ref[...] = jnp.where(first, zero, kk)
    va_ref[...] = jnp.where(first, vv, zero)
    vb_ref[...] = jnp.where(first, zero, vv)

    row = lax.broadcasted_iota(jnp.int32, (blk, LANES), 0)
    col = lax.broadcasted_iota(jnp.int32, (blk, LANES), 1)
    strict = (col & (blk - 1)) < row
    src = lax.broadcasted_iota(jnp.int32, (LANES, 2 * LANES), 0)
    dst = lax.broadcasted_iota(jnp.int32, (LANES, 2 * LANES), 1)
    same_head = (src & blk) == (dst & blk)
    later_and_total = (same_head & ((dst >= LANES) | ((src & (blk - 1)) > (dst & (blk - 1))))
                       ).astype(BF16)

    def keys_values(keys):
        return (jnp.concatenate([ka_ref[keys, :], kb_ref[keys, :]], axis=0),
                jnp.concatenate([va_ref[keys, :], vb_ref[keys, :]], axis=0))

    def mask_first_block(x):
        masked = jnp.where(strict, x[:blk], 0.0)
        return masked if x.shape[0] == blk else jnp.concatenate([masked, x[blk:]], axis=0)

    def scores_stage(q_rows, k2, diagonal):
        z = lax.dot_general(q_rows, k2, _NT, preferred_element_type=F32)
        sp = _softplus_bits(z)
        spm = mask_first_block(sp) if diagonal else sp
        return z - sp, spm.astype(BF16)

    def sums_stage(spm):
        return jnp.dot(spm, later_and_total, preferred_element_type=F32)

    def weights_stage(log_beta, sums, rem, v2, diagonal):
        w = jnp.exp2(log_beta - sums[:, :LANES] - rem)
        if diagonal:
            w = mask_first_block(w)
        return jnp.dot(w.astype(BF16), v2, preferred_element_type=F32)

    def block_step(q_rows, k2, v2, rem):
        log_beta, spm = scores_stage(q_rows, k2, False)
        sums = sums_stage(spm)
        return weights_stage(log_beta, sums, rem, v2, False), sums[:, LANES:]

    span_lo = SB_PASS1_ROWS[1:] + (0,)
    carried = {}
    least_rem = {}
    key_blocks = list(range(nb - 1, -1, -1))
    keys_of = lambda kb: slice(kb * blk, (kb + 1) * blk)

    def rows_of(kb):
        offsets = [j for j in range(len(SB_PASS1_ROWS)) if kb + j < nb]
        return slice(kb * blk, (kb + offsets[-1]) * blk + SB_PASS1_ROWS[offsets[-1]]), offsets

    for g0 in range(0, nb, SB_GROUP):
        group = key_blocks[g0:g0 + SB_GROUP]
        operands = [keys_values(keys_of(kb)) for kb in group]
        staged = [scores_stage(q_ref[0,rows_of(kb)[0], :], k2, True)
                  for kb, (k2, _) in zip(group, operands)]
        sums = [sums_stage(spm) for _, spm in staged]
        for kb, (_, v2), (log_beta, _), unit_sums in zip(group, operands, staged, sums):
            rows, offsets = rows_of(kb)
            rem = jnp.concatenate([jnp.zeros((blk, LANES), F32)] + [carried[j] for j in offsets[1:]],
                                  axis=0)
            out = weights_stage(log_beta, unit_sums, rem, v2, True)
            acc_ref[keys_of(kb), :] = out[:blk]
            if len(offsets) > 1:
                acc_ref[rows.start + blk:rows.stop, :] += out[blk:]
            after = rem + unit_sums[:, LANES:]
            for j in offsets:
                rem_j = after[j * blk:j * blk + SB_PASS1_ROWS[j]]
                if j + 1 < len(SB_PASS1_ROWS):
                    carried[j + 1] = rem_j[:SB_PASS1_ROWS[j + 1]]
                if kb > 0 and span_lo[j] < SB_PASS1_ROWS[j]:
                    done = rem_j[span_lo[j]:]
                    rem_ref[(kb + j) * blk + span_lo[j]:(kb + j) * blk + SB_PASS1_ROWS[j], :] = done
                    least_rem[j] = jnp.minimum(least_rem[j], done) if j in least_rem else done

    def need_more(rem):
        return (jnp.min(rem) < SB_ZERO_WEIGHT).astype(jnp.int32)

    def continue_rows(j):
        n_rows = SB_PASS1_ROWS[j] - span_lo[j]

        def query_block(i, _):
            q_rows = pl.ds(pl.multiple_of(i * blk + span_lo[j], BF16_SUBLANES), n_rows)

            def body(c):
                kb, _, rem = c
                k2, v2 = keys_values(pl.ds(pl.multiple_of(kb * blk, blk), blk))
                out, total = block_step(q_ref[0, q_rows, :], k2, v2, rem)
                acc_ref[q_rows, :] += out
                rem = rem + total
                return kb - 1, need_more(rem), rem

            rem = rem_ref[q_rows, :]
            lax.while_loop(lambda c: (c[0] >= 0) & (c[1] > 0), body,
                           (i - j - 1, need_more(rem), rem))
            return 0

        lax.fori_loop(j + 1, nb, query_block, 0)

    for j, least in least_rem.items():
        @pl.when(jnp.min(least) < SB_ZERO_WEIGHT)
        def _():
            continue_rows(j)

    o_ref[0] = acc_ref[...].astype(BF16)


def _sb_attn(q, k, v):
    b, s, d = q.shape
    pairs = d // LANES
    blockspec = pl.BlockSpec((1, s, LANES), lambda i, j: (i, 0, j))
    return pl.pallas_call(
        _sb_attn_body,
        out_shape=jax.ShapeDtypeStruct((b, s, d), BF16),
        grid=(b, pairs),
        in_specs=[blockspec, blockspec, blockspec],
        out_specs=blockspec,
        scratch_shapes=[pltpu.VMEM((s, LANES), BF16)] * 4 + [pltpu.VMEM((s, LANES), F32)] * 2,
        compiler_params=_params(2),
        name="sb_attn",
    )(q, k, v)


def _sb_mixer(x, batch, attn_g, w_in, layer, gq, gk):
    t, d = x.shape
    reps = LANES // SB_HEAD_DIM
    q, k, v = _sb_inproj(x, attn_g, w_in, layer,
                         jnp.tile(gq, reps)[None, :], jnp.tile(gk, reps)[None, :])
    shape3 = (batch, t // batch, d)
    o = _sb_attn(q.reshape(shape3), k.reshape(shape3), v.reshape(shape3))
    return o.reshape(t, d)


def _gla_inproj_body(x_ref, g_ref, w_ref, wg1_ref, wg2_ref, bg_ref,
                     q_ref, k_ref, v_ref, r_ref, la_ref, *, kd, vd, q_scale):
    h = _rmsnorm(x_ref[...], g_ref[...]).astype(BF16)
    low_rank = jnp.dot(h, wg1_ref[...], preferred_element_type=F32).astype(BF16)
    gate = jnp.dot(low_rank, wg2_ref[...], preferred_element_type=F32) + bg_ref[...]
    la_ref[...] = gate
    proj = jnp.dot(h, w_ref[...], preferred_element_type=F32)
    q_ref[...] = (proj[:, :kd] * q_scale).astype(BF16)
    k_ref[...] = proj[:, kd:2 * kd].astype(BF16)
    v_ref[...] = proj[:, 2 * kd:2 * kd + vd].astype(BF16)
    r_ref[...] = proj[:, 2 * kd + vd:].astype(BF16)


def _gla_inproj(x, g, w, layer, wg1, wg2, bg, kd, vd):
    t, d = x.shape
    body = functools.partial(_gla_inproj_body, kd=kd, vd=vd,
                             q_scale=(kd // GLA_HEADS) ** -0.5)
    return pl.pallas_call(
        body,
        out_shape=(jax.ShapeDtypeStruct((t, kd), BF16), jax.ShapeDtypeStruct((t, kd), BF16),
                   jax.ShapeDtypeStruct((t, vd), BF16), jax.ShapeDtypeStruct((t, vd), BF16),
                   jax.ShapeDtypeStruct((t, kd), F32)),
        grid=(t // INPROJ_ROW_TILE,),
        in_specs=[_rows(d, INPROJ_ROW_TILE), _resident((1, d)), _layer_of(w, layer, 2 * kd + 2 * vd),
                  _resident(wg1.shape), _resident(wg2.shape), _resident((1, kd))],
        out_specs=tuple(_rows(width, INPROJ_ROW_TILE) for width in (kd, kd, vd, vd, kd)),
        compiler_params=_params(1),
        name="gla_inproj",
    )(x, g, w, wg1, wg2, bg)


def _gla_scan_body(q_ref, k_ref, la_ref, v_ref, o_ref, rowbuf_ref):
    hk = q_ref.shape[2] // GLA_HEADS_PER_STEP
    hv = v_ref.shape[2] // GLA_HEADS_PER_STEP
    for h in range(GLA_HEADS_PER_STEP):
        _gla_scan_head(q_ref, k_ref, la_ref, v_ref, o_ref, rowbuf_ref,
                       slice(h * hk, (h + 1) * hk), slice(h * hv, (h + 1) * hv))


def _gla_scan_head(q_ref, k_ref, la_ref, v_ref, o_ref, rowbuf_ref, key_lanes, value_lanes):
    seq = q_ref.shape[1]
    c = GLA_CHUNK
    row = lax.broadcasted_iota(jnp.int32, (c, c), 0)
    col = lax.broadcasted_iota(jnp.int32, (c, c), 1)
    causal = col <= row
    prefix = causal.astype(BF16)
    chunks = [slice(n * c, (n + 1) * c) for n in range(seq // c)]
    hk = key_lanes.stop - key_lanes.start

    def chunk_cum(gate):
        la = -(jnp.maximum(-gate, 0.0) + jnp.log(1.0 + jnp.exp(-jnp.abs(gate)))) / GLA_TAU
        la_hi, la_lo = _split_bf16(la)
        both = jnp.dot(prefix, jnp.concatenate([la_hi, la_lo], axis=1), preferred_element_type=F32)
        return both[:, :hk] + both[:, hk:]

    def factored_scores(qd, kf, cum):
        kd = (kf * jnp.exp(jnp.minimum(-cum, GLA_SAFE_DECAY))).astype(BF16)
        scores = lax.dot_general(qd, kd, _NT, preferred_element_type=F32)
        return jnp.where(causal, scores, 0.0).astype(BF16)

    cums = [chunk_cum(la_ref[0, rows, key_lanes]) for rows in chunks]
    qds, k_ends, decays, intra = [], [], [], []
    lowest = cums[0][c - 1:c, :]
    for rows, cum in zip(chunks, cums):
        qf = q_ref[0, rows, key_lanes].astype(F32)
        kf = k_ref[0, rows, key_lanes].astype(F32)
        last = cum[c - 1:c, :]
        lowest = jnp.minimum(lowest, last)
        qd = (qf * jnp.exp(cum)).astype(BF16)
        qds.append(qd)
        k_ends.append((kf * jnp.exp(last - cum)).astype(BF16))
        decays.append(jnp.exp(last))
        intra.append(jnp.dot(factored_scores(qd, kf, cum), v_ref[0, rows, value_lanes],
                             preferred_element_type=F32))
    updates = [lax.dot_general(v_ref[0, rows, value_lanes], k_end, _TN, preferred_element_type=F32)
               for rows, k_end in zip(chunks, k_ends)]
    state = None
    for n, rows in enumerate(chunks):
        out = intra[n]
        if state is not None:
            out = out + lax.dot_general(qds[n], state.astype(BF16), _NT, preferred_element_type=F32)
        o_ref[0, rows, value_lanes] = out.astype(o_ref.dtype)
        state = updates[n] if state is None else state * decays[n] + updates[n]

    def redo_chunk(n, _):
        rows = pl.ds(pl.multiple_of(n * c, c), c)
        cum = chunk_cum(la_ref[0, rows, key_lanes])

        @pl.when(jnp.min(cum[c - 1:c, :]) < -GLA_SAFE_DECAY)
        def _():
            qf = q_ref[0, rows, key_lanes].astype(F32)
            kf = k_ref[0, rows, key_lanes].astype(F32)
            vv = v_ref[0, rows, value_lanes]
            rowbuf_ref[0] = kf
            rowbuf_ref[1] = cum

            def key_column(s, scores):
                k_s = rowbuf_ref[0, pl.ds(s, 1), :]
                cum_s = rowbuf_ref[1, pl.ds(s, 1), :]
                terms = qf * jnp.exp(jnp.minimum(cum - cum_s, 0.0)) * k_s
                column = jnp.sum(terms, axis=-1, keepdims=True)
                return jnp.where((col == s) & (row >= s), column, scores)

            exact = lax.fori_loop(0, c, key_column, jnp.zeros((c, c), F32)).astype(BF16)
            factored = factored_scores((qf * jnp.exp(cum)).astype(BF16), kf, cum)
            correction = (jnp.dot(exact, vv, preferred_element_type=F32)
                          - jnp.dot(factored, vv, preferred_element_type=F32))
            o_ref[0, rows, value_lanes] = (o_ref[0, rows, value_lanes].astype(F32)
                                           + correction).astype(o_ref.dtype)
        return 0

    @pl.when(jnp.min(lowest) < -GLA_SAFE_DECAY)
    def _():
        lax.fori_loop(0, len(chunks), redo_chunk, 0)


def _gla_scan(q, k, la, v):
    b, s, kd = q.shape
    vd = v.shape[2]
    steps = GLA_HEADS // GLA_HEADS_PER_STEP
    key_spec = pl.BlockSpec((1, s, kd // steps), lambda i, j: (i, 0, j))
    val_spec = pl.BlockSpec((1, s, vd // steps), lambda i, j: (i, 0, j))
    return pl.pallas_call(
        _gla_scan_body,
        out_shape=jax.ShapeDtypeStruct((b, s, vd), BF16),
        grid=(b, steps),
        in_specs=[key_spec, key_spec, key_spec, val_spec],
        out_specs=val_spec,
        scratch_shapes=[pltpu.VMEM((2, GLA_CHUNK, kd // GLA_HEADS), F32)],
        compiler_params=_params(2),
        name="gla_scan",
    )(q, k, la, v)


def _gla_mixer(x, batch, attn_g, w_in, layer, w_gate2, b_gate, vd):
    t, d = x.shape
    kd = w_gate2.shape[1]
    main = 2 * kd + 2 * vd
    wg1 = jnp.pad(w_in[layer, :, main:], ((0, 0), (0, LANES - GLA_GATE_RANK)))
    wg2 = jnp.pad(w_gate2, ((0, LANES - GLA_GATE_RANK), (0, 0)))
    q, k, v, r, la = _gla_inproj(x, attn_g, w_in, layer, wg1, wg2, b_gate[None, :], kd, vd)
    s = t // batch
    o = _gla_scan(q.reshape(batch, s, kd), k.reshape(batch, s, kd), la.reshape(batch, s, kd),
                  v.reshape(batch, s, vd))
    return o.reshape(t, vd), r


def kernel(x, attn_norm_g, ffn_norm_g, gla_w_in, gla_w_gate2, gla_b_gate, gla_norm_g, gla_w_out,
           sb_w_in, sb_q_norm_g, sb_k_norm_g, sb_w_out, ffn_w_gate_up, ffn_w_down):
    batch, seq, d = x.shape
    depth = attn_norm_g.shape[0]
    assert (batch * seq) % ROW_TILE == 0 and (batch * seq) % INPROJ_ROW_TILE == 0
    assert seq % GLA_CHUNK == 0 and seq % SB_BLOCK == 0
    gla_w_in, gla_w_gate2, gla_w_out, sb_w_in, sb_w_out, ffn_w_gate_up, ffn_w_down = (
        w.astype(BF16) for w in (gla_w_in, gla_w_gate2, gla_w_out, sb_w_in, sb_w_out,
                                 ffn_w_gate_up, ffn_w_down))
    xt = x.reshape(batch * seq, d)
    for i in range(depth):
        j = i // 2
        attn_g = attn_norm_g[i][None, :]
        ffn = (ffn_norm_g[i][None, :], ffn_w_gate_up, ffn_w_down, i)
        if i % 2 == 0:
            o, r = _gla_mixer(xt, batch, attn_g, gla_w_in, j, gla_w_gate2[j], gla_b_gate[j],
                              gla_w_out.shape[1])
            xt = _gla_out_ffn(xt, o, r, gla_norm_g[j][None, :], gla_w_out, j, *ffn)
        else:
            o = _sb_mixer(xt, batch, attn_g, sb_w_in, j, sb_q_norm_g[j], sb_k_norm_g[j])
            xt = _sb_out_ffn(xt, o, sb_w_out, j, *ffn)
    return xt.reshape(batch, seq, d)
```
